```python
import jax, jax.numpy as jnp
from jax import lax
import numpy as np

D_MODEL = 1024
BATCH = 4
SEQ = 8192
DEPTH = 1

MIX_WIDTH = D_MODEL
A_HEADS = 4
A_WIDTH = MIX_WIDTH // 2
A_HEAD_DIM = A_WIDTH // A_HEADS
CHUNK = 128
B_WIDTH = MIX_WIDTH - A_WIDTH
CONV_WIDTH = 3
IN_WIDTH = 2 * A_WIDTH + 3 * B_WIDTH
MEM_LEN = 256
X_HEADS = 4
X_HEAD_DIM = D_MODEL // X_HEADS
PEER_HEADS = 8
N_KEYS = 128
N_EXPERTS = N_KEYS * N_KEYS
PEER_TOPK = 16
D_KEY = 256
HALF_KEY = D_KEY // 2
PEER_BLOCK = 128
LN_EPS = 1e-5
DN_ALPHA = (2 * DEPTH) ** 0.25
DN_BETA = (8 * DEPTH) ** -0.25

kernel_name = "hybrid_sgu_shortconv_xattn_peer_deepnorm"


def _layernorm(x, g, b):
    xf = x.astype(jnp.float32)
    mu = jnp.mean(xf, axis=-1, keepdims=True)
    var = jnp.mean(jnp.square(xf - mu), axis=-1, keepdims=True)
    y = (xf - mu) * lax.rsqrt(var + LN_EPS) * g.astype(jnp.float32) + b.astype(jnp.float32)
    return y.astype(x.dtype)


def _gelu(x):
    return jax.nn.gelu(x, approximate=False)


def _hybrid_mixer(x, w_in, b_in, sgu_g, sgu_b, w_spatial, b_spatial, conv_w, w_out):
    bsz, seq, _ = x.shape
    z = jnp.einsum('bsd,de->bse', x, w_in) + b_in
    u_a, v_a, h_b, gate_b, gate_c = jnp.split(
        z, [A_WIDTH, 2 * A_WIDTH, 2 * A_WIDTH + B_WIDTH, 2 * A_WIDTH + 2 * B_WIDTH], axis=-1)
    u_a = _gelu(u_a)
    v_a = _gelu(v_a).reshape(bsz, seq // CHUNK, CHUNK, A_HEADS, A_HEAD_DIM)
    v_a = _layernorm(v_a, sgu_g, sgu_b)
    causal = jnp.tril(jnp.ones((CHUNK, CHUNK), dtype=bool))
    ws = jnp.where(causal[None], w_spatial, 0.0)
    mixed = jnp.einsum('hts,bcshd->bcthd', ws, v_a) + b_spatial.T[:, :, None]
    y_a = u_a * mixed.reshape(bsz, seq, A_WIDTH)
    zc = gate_c * h_b
    conv = lax.conv_general_dilated(
        zc, conv_w, window_strides=(1,), padding=((CONV_WIDTH - 1, 0),),
        dimension_numbers=('NWC', 'WIO', 'NWC'), feature_group_count=B_WIDTH)
    y_b = gate_b * conv
    y = jnp.concatenate([y_a, y_b], axis=-1)
    return jnp.einsum('bse,ed->bsd', y, w_out)


def _cross_attention(x, mem, w_q, w_kv, w_o):
    bsz, seq, _ = x.shape
    q = jnp.einsum('bsd,de->bse', x, w_q).reshape(bsz, seq, X_HEADS, X_HEAD_DIM)
    kv = jnp.einsum('bmd,de->bme', mem, w_kv).reshape(bsz, mem.shape[1], 2, X_HEADS, X_HEAD_DIM)
    k, v = kv[:, :, 0], kv[:, :, 1]
    scores = jnp.einsum('bshk,bmhk->bhsm', q, k).astype(jnp.float32) * (X_HEAD_DIM ** -0.5)
    p = jax.nn.softmax(scores, axis=-1).astype(x.dtype)
    o = jnp.einsum('bhsm,bmhk->bshk', p, v).reshape(bsz, seq, D_MODEL)
    return jnp.einsum('bsd,de->bse', o, w_o)


def _peer(x, w_query, sub_keys, w_down, w_up):
    bsz, seq, _ = x.shape
    n_tok = bsz * seq
    xt = x.reshape(n_tok, D_MODEL)
    q = jnp.einsum('td,de->te', xt, w_query).reshape(n_tok, PEER_HEADS, 2, HALF_KEY)
    sc = jnp.einsum('thpk,hpnk->thpn', q, sub_keys).astype(jnp.float32)
    va, ia = lax.top_k(sc[:, :, 0], PEER_TOPK)
    vb, ib = lax.top_k(sc[:, :, 1], PEER_TOPK)
    cand = (va[..., :, None] + vb[..., None, :]).reshape(n_tok, PEER_HEADS, PEER_TOPK * PEER_TOPK)
    top, flat = lax.top_k(cand, PEER_TOPK)
    expert = (jnp.take_along_axis(ia, flat // PEER_TOPK, axis=-1) * N_KEYS
              + jnp.take_along_axis(ib, flat % PEER_TOPK, axis=-1))
    gate = jax.nn.softmax(top, axis=-1).astype(x.dtype)
    n_blk = n_tok // PEER_BLOCK
    hk = PEER_HEADS * PEER_TOPK
    expert = expert.reshape(n_blk, PEER_BLOCK, hk)
    gate = gate.reshape(n_blk, PEER_BLOCK, hk)

    def block(args):
        xb, eb, gb = args
        act = _gelu(jnp.einsum('tkd,td->tk', w_down[eb], xb))
        return jnp.einsum('tk,tkd->td', gb * act, w_up[eb])

    out = lax.map(block, (xt.reshape(n_blk, PEER_BLOCK, D_MODEL), expert, gate))
    return out.reshape(bsz, seq, D_MODEL)


def _normal(k, shape, scale):
    return jax.random.normal(k, shape, jnp.float32) * scale


def setup_inputs(seed: int = 0) -> dict:
    key = jax.random.key(seed)
    ks = jax.random.split(key, 24)
    L, D = DEPTH, D_MODEL
    return {
        "x": _normal(ks[0], (BATCH, SEQ, D), 1.0),
        "mem": _normal(ks[1], (BATCH, MEM_LEN, D), 1.0),
        "w_in": _normal(ks[2], (L, D, IN_WIDTH), D ** -0.5),
        "b_in": _normal(ks[3], (L, IN_WIDTH), 0.02),
        "sgu_g": 1.0 + _normal(ks[4], (L, A_HEADS, A_HEAD_DIM), 0.02),
        "sgu_b": _normal(ks[5], (L, A_HEADS, A_HEAD_DIM), 0.02),
        "w_spatial": _normal(ks[6], (L, A_HEADS, CHUNK, CHUNK), CHUNK ** -0.5),
        "b_spatial": 1.0 + _normal(ks[7], (L, A_HEADS, CHUNK), 0.02),
        "conv_w": _normal(ks[8], (L, CONV_WIDTH, 1, B_WIDTH), CONV_WIDTH ** -0.5),
        "w_mix_out": _normal(ks[9], (L, MIX_WIDTH, D), MIX_WIDTH ** -0.5 * DN_BETA),
        "ln1_g": 1.0 + _normal(ks[10], (L, D), 0.02),
        "ln1_b": _normal(ks[11], (L, D), 0.02),
        "w_xq": _normal(ks[12], (L, D, D), D ** -0.5),
        "w_xkv": jnp.concatenate([_normal(ks[13], (L, D, D), D ** -0.5),
                                  _normal(ks[14], (L, D, D), D ** -0.5 * DN_BETA)], axis=-1),
        "w_xo": _normal(ks[15], (L, D, D), D ** -0.5 * DN_BETA),
        "ln2_g": 1.0 + _normal(ks[16], (L, D), 0.02),
        "ln2_b": _normal(ks[17], (L, D), 0.02),
        "w_pq": _normal(ks[18], (L, D, PEER_HEADS * D_KEY), D ** -0.5),
        "sub_keys": _normal(ks[19], (L, PEER_HEADS, 2, N_KEYS, HALF_KEY), HALF_KEY ** -0.5),
        "w_down": _normal(ks[20], (L, N_EXPERTS, D), D ** -0.5),
        "w_up": _normal(ks[21], (L, N_EXPERTS, D), DN_BETA),
        "ln3_g": 1.0 + _normal(ks[22], (L, D), 0.02),
        "ln3_b": _normal(ks[23], (L, D), 0.02),
    }


def reference(x, mem, w_in, b_in, sgu_g, sgu_b, w_spatial, b_spatial, conv_w, w_mix_out,
              ln1_g, ln1_b, w_xq, w_xkv, w_xo, ln2_g, ln2_b, w_pq, sub_keys, w_down, w_up,
              ln3_g, ln3_b):
    for l in range(DEPTH):
        h = _hybrid_mixer(x, w_in[l], b_in[l], sgu_g[l], sgu_b[l], w_spatial[l], b_spatial[l],
                          conv_w[l], w_mix_out[l])
        x = _layernorm(DN_ALPHA * x + h, ln1_g[l], ln1_b[l])
        h = _cross_attention(x, mem, w_xq[l], w_xkv[l], w_xo[l])
        x = _layernorm(DN_ALPHA * x + h, ln2_g[l], ln2_b[l])
        h = _peer(x, w_pq[l], sub_keys[l], w_down[l], w_up[l])
        x = _layernorm(DN_ALPHA * x + h, ln3_g[l], ln3_b[l])
    return x
```

```python
import functools

import jax
import jax.numpy as jnp
import numpy as np
from jax import lax
from jax.experimental import pallas as pl
from jax.experimental.pallas import tpu as pltpu

LN_EPS = 1e-5
LANES = 128
SUBLANES = 8
A_HEADS = 4
CHUNK = 128
X_HEADS = 4
PEER_HEADS = 8
N_KEYS = 128
TOPK = 16
VMEM_LIMIT = 56 * 1024 * 1024

F32 = jnp.float32
BF16 = jnp.bfloat16


def _gelu(x):
    return 0.5 * x * (1.0 + lax.erf(x * np.float32(2.0 ** -0.5)))


def _ln(v, g, b):
    mu = jnp.mean(v, axis=-1, keepdims=True)
    d = v - mu
    var = jnp.mean(d * d, axis=-1, keepdims=True)
    return d * lax.rsqrt(var + LN_EPS) * g + b


def _dot(a, b):
    return jnp.dot(a, b, preferred_element_type=F32)


def _dot_nt(a, b):
    return lax.dot_general(a, b, (((1,), (1,)), ((), ())), preferred_element_type=F32)


def _mixer_kernel(x_ref, win_ref, bin_ref, sg_ref, sb_ref, ws_ref, bsp_ref, cw_ref, wout_ref,
                  lg_ref, lb_ref, o_ref, zc_ref, *, alpha, ts, aw):
    @pl.when(pl.program_id(1) == 0)
    def _():
        zc_ref[0:SUBLANES, :] = jnp.zeros((SUBLANES, zc_ref.shape[1]), F32)

    x = x_ref[0]
    xb = x.astype(BF16)

    def zcol(lo, hi):
        return _dot(xb, win_ref[:, lo:hi]) + bin_ref[:, lo:hi]

    hd = aw // A_HEADS
    u = _gelu(zcol(0, aw))
    v = _gelu(zcol(aw, 2 * aw))
    mixed = []
    for h in range(A_HEADS):
        sl = slice(h * hd, (h + 1) * hd)
        vh = _ln(v[:, sl], sg_ref[:, sl], sb_ref[:, sl]).astype(BF16)
        mixed.append(jnp.concatenate(
            [_dot(ws_ref[h], vh[c * CHUNK:(c + 1) * CHUNK]) + bsp_ref[:, sl]
             for c in range(ts // CHUNK)], axis=0))
    ya = u * jnp.concatenate(mixed, axis=-1)

    bw = zc_ref.shape[1]
    hb = zcol(2 * aw, 2 * aw + bw)
    gb = zcol(2 * aw + bw, 2 * aw + 2 * bw)
    gc = zcol(2 * aw + 2 * bw, 2 * aw + 3 * bw)
    zc_ref[SUBLANES:SUBLANES + ts, :] = gc * hb
    conv = (cw_ref[0:1, :] * zc_ref[SUBLANES - 2:SUBLANES - 2 + ts, :]
            + cw_ref[1:2, :] * zc_ref[SUBLANES - 1:SUBLANES - 1 + ts, :]
            + cw_ref[2:3, :] * zc_ref[SUBLANES:SUBLANES + ts, :])
    yb = gb * conv
    zc_ref[0:SUBLANES, :] = zc_ref[ts:ts + SUBLANES, :]

    h = _dot(ya.astype(BF16), wout_ref[0:aw, :]) + _dot(yb.astype(BF16), wout_ref[aw:, :])
    o_ref[0] = _ln(alpha * x + h, lg_ref[...], lb_ref[...])


def _mixer(x, w_in, b_in, sgu_g, sgu_b, w_spatial, b_spatial, conv_w, w_out, ln_g, ln_b, alpha, ts):
    bsz, seq, d = x.shape
    aw = A_HEADS * sgu_g.shape[-1]
    bw = conv_w.shape[-1]
    causal = jnp.tril(jnp.ones((CHUNK, CHUNK), dtype=bool))
    ws = jnp.where(causal[None], w_spatial, 0.0).astype(BF16)
    bsp = jnp.repeat(b_spatial.T, aw // A_HEADS, axis=1)
    const = lambda *shape: pl.BlockSpec(shape, lambda b, i: (0,) * len(shape))
    return pl.pallas_call(
        functools.partial(_mixer_kernel, alpha=alpha, ts=ts, aw=aw),
        grid=(bsz, seq // ts),
        in_specs=[
            pl.BlockSpec((1, ts, d), lambda b, i: (b, i, 0)),
            const(d, w_in.shape[1]), const(1, w_in.shape[1]), const(1, aw), const(1, aw),
            const(A_HEADS, CHUNK, CHUNK), const(CHUNK, aw), const(conv_w.shape[0], bw),
            const(aw + bw, d), const(1, d), const(1, d),
        ],
        out_specs=pl.BlockSpec((1, ts, d), lambda b, i: (b, i, 0)),
        out_shape=jax.ShapeDtypeStruct(x.shape, F32),
        scratch_shapes=[pltpu.VMEM((ts + 2 * SUBLANES, bw), F32)],
        compiler_params=pltpu.CompilerParams(
            dimension_semantics=("parallel", "arbitrary"), vmem_limit_bytes=VMEM_LIMIT),
        name="mixer",
    )(x, w_in.astype(BF16), b_in.reshape(1, -1), sgu_g.reshape(1, -1), sgu_b.reshape(1, -1),
      ws, bsp, conv_w.reshape(conv_w.shape[0], bw), w_out.astype(BF16),
      ln_g.reshape(1, -1), ln_b.reshape(1, -1))


def _kv_kernel(m_ref, w_ref, o_ref):
    o_ref[0] = _dot(m_ref[0].astype(BF16), w_ref[...]).astype(BF16)


def _kv_proj(mem, w_kv):
    bsz, m, d = mem.shape
    n = w_kv.shape[1]
    return pl.pallas_call(
        _kv_kernel,
        grid=(bsz,),
        in_specs=[pl.BlockSpec((1, m, d), lambda b: (b, 0, 0)),
                  pl.BlockSpec((d, n), lambda b: (0, 0))],
        out_specs=pl.BlockSpec((1, m, n), lambda b: (b, 0, 0)),
        out_shape=jax.ShapeDtypeStruct((bsz, m, n), BF16),
        compiler_params=pltpu.CompilerParams(
            dimension_semantics=("parallel",), vmem_limit_bytes=VMEM_LIMIT),
        name="kv_proj",
    )(mem, w_kv.astype(BF16))


def _xattn_kernel(x_ref, kv_ref, wq_ref, wo_ref, lg_ref, lb_ref, o_ref, *, alpha):
    x = x_ref[0]
    d = x.shape[-1]
    dk = d // X_HEADS
    q = _dot(x.astype(BF16), wq_ref[...])
    outs = []
    for h in range(X_HEADS):
        qh = q[:, h * dk:(h + 1) * dk].astype(BF16)
        s = _dot_nt(qh, kv_ref[0, :, h * dk:(h + 1) * dk]) * np.float32(dk ** -0.5)
        e = jnp.exp(s - jnp.max(s, axis=-1, keepdims=True))
        p = e / jnp.sum(e, axis=-1, keepdims=True)
        outs.append(_dot(p.astype(BF16), kv_ref[0, :, d + h * dk:d + (h + 1) * dk]))
    o = jnp.concatenate(outs, axis=-1).astype(BF16)
    o_ref[0] = _ln(alpha * x + _dot(o, wo_ref[...]), lg_ref[...], lb_ref[...])


def _xattn(x, kv, w_q, w_o, ln_g, ln_b, alpha, ts):
    bsz, seq, d = x.shape
    m = kv.shape[1]
    return pl.pallas_call(
        functools.partial(_xattn_kernel, alpha=alpha),
        grid=(bsz, seq // ts),
        in_specs=[
            pl.BlockSpec((1, ts, d), lambda b, i: (b, i, 0)),
            pl.BlockSpec((1, m, 2 * d), lambda b, i: (b, 0, 0)),
            pl.BlockSpec((d, d), lambda b, i: (0, 0)),
            pl.BlockSpec((d, d), lambda b, i: (0, 0)),
            pl.BlockSpec((1, d), lambda b, i: (0, 0)),
            pl.BlockSpec((1, d), lambda b, i: (0, 0)),
        ],
        out_specs=pl.BlockSpec((1, ts, d), lambda b, i: (b, i, 0)),
        out_shape=jax.ShapeDtypeStruct(x.shape, F32),
        compiler_params=pltpu.CompilerParams(
            dimension_semantics=("parallel", "parallel"), vmem_limit_bytes=VMEM_LIMIT),
        name="xattn",
    )(x, kv, w_q.astype(BF16), w_o.astype(BF16), ln_g.reshape(1, -1), ln_b.reshape(1, -1))


def _extract_top(vals, tags, payload, rounds):
    big = jnp.int32(2 ** 30)
    tops, picks = [], []
    for _ in range(rounds):
        m = jnp.max(vals, axis=0, keepdims=True)
        first = jnp.min(jnp.where(vals == m, tags, big), axis=0, keepdims=True)
        sel = tags == first
        tops.append(m)
        picks.append(jnp.max(jnp.where(sel, payload, -1), axis=0, keepdims=True))
        vals = jnp.where(sel, -jnp.inf, vals)
    return jnp.concatenate(tops, axis=0), jnp.concatenate(picks, axis=0)


def _route_kernel(x_ref, wq_ref, keys_ref, code_ref, idx_ref, gate_ref, idx_t, gate_t):
    tt = x_ref.shape[0]
    xb = x_ref[...].astype(BF16)
    row = lax.broadcasted_iota(jnp.int32, (N_KEYS, tt), 0)
    code = code_ref[...]

    def head(h, carry):
        q = _dot(xb, wq_ref[h]).astype(BF16)
        hk = q.shape[1] // 2
        sa = _dot_nt(keys_ref[h, 0], q[:, :hk])
        sb = _dot_nt(keys_ref[h, 1], q[:, hk:])
        va, ia = _extract_top(sa, row, row, TOPK)
        vb, ib = _extract_top(sb, row, row, TOPK)
        ea = ia * N_KEYS
        cand = [va + vb[0:1], ]
        eid = [ea + ib[0:1], ]
        for j in range(1, SUBLANES):
            cand.append(va[0:SUBLANES] + vb[j:j + 1])
            eid.append(ea[0:SUBLANES] + ib[j:j + 1])
        cand.append(va[0:1] + vb[SUBLANES:])
        eid.append(ea[0:1] + ib[SUBLANES:])
        cand = jnp.where(code >= 0, jnp.concatenate(cand, axis=0), -jnp.inf)
        eid = jnp.concatenate(eid, axis=0)
        top, expert = _extract_top(cand, code, eid, TOPK)
        e = jnp.exp(top - top[0:1])
        r0 = pl.multiple_of(h * TOPK, TOPK)
        gate_t[pl.ds(r0, TOPK), :] = e / jnp.sum(e, axis=0, keepdims=True)
        idx_t[pl.ds(r0, TOPK), :] = expert
        return carry

    lax.fori_loop(0, PEER_HEADS, head, 0)
    idx_ref[...] = idx_t[...].T
    gate_ref[...] = gate_t[...].T


def _candidate_codes(tt):
    codes = [[i * TOPK for i in range(TOPK)]]
    for j in range(1, SUBLANES):
        codes.append([i * TOPK + j if (i + 1) * (j + 1) <= TOPK else -1 for i in range(SUBLANES)])
    codes.append(list(range(SUBLANES, TOPK)))
    flat = np.asarray([c for blk in codes for c in blk], dtype=np.int32)
    return jnp.asarray(np.broadcast_to(flat[:, None], (flat.shape[0], tt)))


def _peer_route(xt, w_query, sub_keys, tt):
    n_tok, d = xt.shape
    dkey = w_query.shape[1] // PEER_HEADS
    wq = w_query.reshape(d, PEER_HEADS, dkey).transpose(1, 0, 2).astype(BF16)
    hk = PEER_HEADS * TOPK
    code = _candidate_codes(tt)
    return pl.pallas_call(
        _route_kernel,
        grid=(n_tok // tt,),
        in_specs=[
            pl.BlockSpec((tt, d), lambda i: (i, 0)),
            pl.BlockSpec(wq.shape, lambda i: (0, 0, 0)),
            pl.BlockSpec(sub_keys.shape, lambda i: (0, 0, 0, 0)),
            pl.BlockSpec(code.shape, lambda i: (0, 0)),
        ],
        out_specs=[pl.BlockSpec((tt, hk), lambda i: (i, 0)),
                   pl.BlockSpec((tt, hk), lambda i: (i, 0))],
        out_shape=[jax.ShapeDtypeStruct((n_tok, hk), jnp.int32),
                   jax.ShapeDtypeStruct((n_tok, hk), F32)],
        scratch_shapes=[pltpu.VMEM((hk, tt), jnp.int32), pltpu.VMEM((hk, tt), F32)],
        compiler_params=pltpu.CompilerParams(
            dimension_semantics=("parallel",), vmem_limit_bytes=VMEM_LIMIT),
        name="peer_route",
    )(xt, wq, sub_keys.astype(BF16), code)


ROW_WORDS = 4
TOK_GROUP = 8


def _pack_rows(w):
    n, d = w.shape
    assert d == 2 * ROW_WORDS * LANES
    bits = lax.bitcast_convert_type(w.astype(BF16), jnp.uint16).astype(jnp.uint32)
    packed = (bits[:, :d // 2] << 16) | bits[:, d // 2:]
    return packed.reshape(n * ROW_WORDS, LANES)


def _unpack(words):
    hi = pltpu.bitcast(words & jnp.uint32(0xFFFF0000), F32)
    lo = pltpu.bitcast(words << 16, F32)
    return hi, lo


def _down_kernel(idx_ref, x_ref, gate_ref, tab_ref, c_ref, prod_ref):
    hk = gate_ref.shape[1]
    ones = jnp.ones((SUBLANES, LANES), BF16)
    rows = lax.broadcasted_iota(jnp.int32, (SUBLANES, hk), 0)

    def group(g, carry):
        t0 = pl.multiple_of(g * TOK_GROUP, TOK_GROUP)
        act = jnp.zeros((SUBLANES, hk), F32)
        for j in range(TOK_GROUP):
            xt = x_ref[t0 + j]
            xa = xt[0:ROW_WORDS]
            xb = xt[ROW_WORDS:]
            for k in range(hk):
                e = pl.multiple_of(idx_ref[t0 + j, k], ROW_WORDS)
                hi, lo = _unpack(tab_ref[pl.ds(e, ROW_WORDS), :])
                prod_ref[k * ROW_WORDS:(k + 1) * ROW_WORDS, :] = hi * xa + lo * xb
            part = prod_ref[pl.ds(0, hk, stride=ROW_WORDS), :]
            for s in range(1, ROW_WORDS):
                part = part + prod_ref[pl.ds(s, hk, stride=ROW_WORDS), :]
            sums = _dot_nt(ones, part.astype(BF16))
            act = jnp.where(rows == j, sums, act)
        c_ref[pl.ds(t0, TOK_GROUP), :] = gate_ref[pl.ds(t0, TOK_GROUP), :] * _gelu(act)
        return carry

    lax.fori_loop(0, x_ref.shape[0] // TOK_GROUP, group, 0)


def _peer_down(idx, x3, gate, tab, tb):
    n_tok, hk = idx.shape
    return pl.pallas_call(
        _down_kernel,
        grid=(n_tok // tb,),
        in_specs=[
            pl.BlockSpec((tb, hk), lambda i: (i, 0), memory_space=pltpu.SMEM),
            pl.BlockSpec((tb, SUBLANES, LANES), lambda i: (i, 0, 0)),
            pl.BlockSpec((tb, hk), lambda i: (i, 0)),
            pl.BlockSpec(memory_space=pltpu.VMEM),
        ],
        out_specs=pl.BlockSpec((tb, hk), lambda i: (i, 0)),
        out_shape=jax.ShapeDtypeStruct((n_tok, hk), F32),
        scratch_shapes=[pltpu.VMEM((hk * ROW_WORDS, LANES), F32)],
        compiler_params=pltpu.CompilerParams(
            dimension_semantics=("arbitrary",), vmem_limit_bytes=VMEM_LIMIT),
        name="peer_down",
    )(idx, x3, gate, tab)


def _up_kernel(idx_ref, c_ref, tab_ref, o_ref, cb_ref):
    hk = c_ref.shape[1]
    eye = (lax.broadcasted_iota(jnp.int32, (hk, hk), 0)
           == lax.broadcasted_iota(jnp.int32, (hk, hk), 1)).astype(F32)
    ones = jnp.ones((hk, LANES), F32)

    def group(g, carry):
        t0 = pl.multiple_of(g * TOK_GROUP, TOK_GROUP)
        cg = c_ref[pl.ds(t0, TOK_GROUP), :]
        for j in range(TOK_GROUP):
            cb_ref[...] = jnp.dot(eye * cg[j:j + 1, :], ones, preferred_element_type=F32,
                                  precision=lax.Precision.HIGHEST)
            acc_hi = jnp.zeros((ROW_WORDS, LANES), F32)
            acc_lo = jnp.zeros((ROW_WORDS, LANES), F32)
            for k in range(hk):
                e = pl.multiple_of(idx_ref[t0 + j, k], ROW_WORDS)
                hi, lo = _unpack(tab_ref[pl.ds(e, ROW_WORDS), :])
                cb = jnp.broadcast_to(cb_ref[k:k + 1, :], (ROW_WORDS, LANES))
                acc_hi = acc_hi + cb * hi
                acc_lo = acc_lo + cb * lo
            o_ref[t0 + j] = jnp.concatenate([acc_hi, acc_lo], axis=0)
        return carry

    lax.fori_loop(0, c_ref.shape[0] // TOK_GROUP, group, 0)


def _peer_up(idx, c, tab, tb):
    n_tok, hk = idx.shape
    return pl.pallas_call(
        _up_kernel,
        grid=(n_tok // tb,),
        in_specs=[
            pl.BlockSpec((tb, hk), lambda i: (i, 0), memory_space=pltpu.SMEM),
            pl.BlockSpec((tb, hk), lambda i: (i, 0)),
            pl.BlockSpec(memory_space=pltpu.VMEM),
        ],
        out_specs=pl.BlockSpec((tb, SUBLANES, LANES), lambda i: (i, 0, 0)),
        out_shape=jax.ShapeDtypeStruct((n_tok, SUBLANES, LANES), F32),
        scratch_shapes=[pltpu.VMEM((hk, LANES), F32)],
        compiler_params=pltpu.CompilerParams(
            dimension_semantics=("arbitrary",), vmem_limit_bytes=VMEM_LIMIT),
        name="peer_up",
    )(idx, c, tab)


def _add_ln_kernel(x_ref, h_ref, g_ref, b_ref, o_ref, *, alpha):
    o_ref[...] = _ln(alpha * x_ref[...] + h_ref[...], g_ref[...], b_ref[...])


def _add_ln(x, h, g, b, alpha, tt):
    n_tok, d = x.shape
    return pl.pallas_call(
        functools.partial(_add_ln_kernel, alpha=alpha),
        grid=(n_tok // tt,),
        in_specs=[pl.BlockSpec((tt, d), lambda i: (i, 0)), pl.BlockSpec((tt, d), lambda i: (i, 0)),
                  pl.BlockSpec((1, d), lambda i: (0, 0)), pl.BlockSpec((1, d), lambda i: (0, 0))],
        out_specs=pl.BlockSpec((tt, d), lambda i: (i, 0)),
        out_shape=jax.ShapeDtypeStruct(x.shape, F32),
        compiler_params=pltpu.CompilerParams(
            dimension_semantics=("parallel",), vmem_limit_bytes=VMEM_LIMIT),
        name="add_ln",
    )(x, h, g.reshape(1, -1), b.reshape(1, -1))


def _tile(n, want):
    return want if n % want == 0 else n


def kernel(x, mem, w_in, b_in, sgu_g, sgu_b, w_spatial, b_spatial, conv_w, w_mix_out, ln1_g, ln1_b,
           w_xq, w_xkv, w_xo, ln2_g, ln2_b, w_pq, sub_keys, w_down, w_up, ln3_g, ln3_b):
    depth = w_in.shape[0]
    bsz, seq, d = x.shape
    assert d == SUBLANES * LANES and seq % CHUNK == 0
    alpha = np.float32((2 * depth) ** 0.25)
    n_tok = bsz * seq
    ts = _tile(seq, 512)
    for l in range(depth):
        x = _mixer(x, w_in[l], b_in[l], sgu_g[l], sgu_b[l], w_spatial[l], b_spatial[l], conv_w[l],
                   w_mix_out[l], ln1_g[l], ln1_b[l], alpha, ts)
        kv = _kv_proj(mem, w_xkv[l])
        x = _xattn(x, kv, w_xq[l], w_xo[l], ln2_g[l], ln2_b[l], alpha, ts)
        xt = x.reshape(n_tok, d)
        idx, gate = _peer_route(xt, w_pq[l], sub_keys[l], _tile(n_tok, 256))
        idx = idx * ROW_WORDS
        tb = _tile(n_tok, 128)
        c = _peer_down(idx, xt.reshape(n_tok, SUBLANES, LANES), gate, _pack_rows(w_down[l]), tb)
        h = _peer_up(idx, c, _pack_rows(w_up[l]), tb)
        x = _add_ln(xt, h.reshape(n_tok, d), ln3_g[l], ln3_b[l], alpha, _tile(n_tok, 512))
        x = x.reshape(bsz, seq, d)
    return x
```

```python
import functools

import jax
import jax.numpy as jnp
import numpy as np
from jax import lax
from jax.experimental import pallas as pl
from jax.experimental.pallas import tpu as pltpu

LN_EPS = 1e-5
LANES = 128
SUBLANES = 8
A_HEADS = 4
CHUNK = 128
X_HEADS = 4
PEER_HEADS = 8
N_KEYS = 128
TOPK = 16
VMEM_LIMIT = 56 * 1024 * 1024

F32 = jnp.float32
BF16 = jnp.bfloat16


def _gelu(x):
    return 0.5 * x * (1.0 + lax.erf(x * np.float32(2.0 ** -0.5)))


def _ln(v, g, b):
    mu = jnp.mean(v, axis=-1, keepdims=True)
    d = v - mu
    var = jnp.mean(d * d, axis=-1, keepdims=True)
    return d * lax.rsqrt(var + LN_EPS) * g + b


def _dot(a, b):
    return jnp.dot(a, b, preferred_element_type=F32)


def _dot_nt(a, b):
    return lax.dot_general(a, b, (((1,), (1,)), ((), ())), preferred_element_type=F32)


def _mixer_kernel(x_ref, win_ref, bin_ref, sg_ref, sb_ref, ws_ref, bsp_ref, cw_ref, wout_ref,
                  lg_ref, lb_ref, o_ref, zc_ref, *, alpha, ts, aw):
    @pl.when(pl.program_id(1) == 0)
    def _():
        zc_ref[0:SUBLANES, :] = jnp.zeros((SUBLANES, zc_ref.shape[1]), F32)

    x = x_ref[0]
    xb = x.astype(BF16)

    def zcol(lo, hi):
        return _dot(xb, win_ref[:, lo:hi]) + bin_ref[:, lo:hi]

    hd = aw // A_HEADS
    u = _gelu(zcol(0, aw))
    v = _gelu(zcol(aw, 2 * aw))
    mixed = []
    for h in range(A_HEADS):
        sl = slice(h * hd, (h + 1) * hd)
        vh = _ln(v[:, sl], sg_ref[:, sl], sb_ref[:, sl]).astype(BF16)
        mixed.append(jnp.concatenate(
            [_dot(ws_ref[h], vh[c * CHUNK:(c + 1) * CHUNK]) + bsp_ref[:, sl]
             for c in range(ts // CHUNK)], axis=0))
    ya = u * jnp.concatenate(mixed, axis=-1)

    bw = zc_ref.shape[1]
    hb = zcol(2 * aw, 2 * aw + bw)
    gb = zcol(2 * aw + bw, 2 * aw + 2 * bw)
    gc = zcol(2 * aw + 2 * bw, 2 * aw + 3 * bw)
    zc_ref[SUBLANES:SUBLANES + ts, :] = gc * hb
    conv = (cw_ref[0:1, :] * zc_ref[SUBLANES - 2:SUBLANES - 2 + ts, :]
            + cw_ref[1:2, :] * zc_ref[SUBLANES - 1:SUBLANES - 1 + ts, :]
            + cw_ref[2:3, :] * zc_ref[SUBLANES:SUBLANES + ts, :])
    yb = gb * conv
    zc_ref[0:SUBLANES, :] = zc_ref[ts:ts + SUBLANES, :]

    h = _dot(ya.astype(BF16), wout_ref[0:aw, :]) + _dot(yb.astype(BF16), wout_ref[aw:, :])
    o_ref[0] = _ln(alpha * x + h, lg_ref[...], lb_ref[...])


def _mixer(x, w_in, b_in, sgu_g, sgu_b, w_spatial, b_spatial, conv_w, w_out, ln_g, ln_b, alpha, ts):
    bsz, seq, d = x.shape
    aw = A_HEADS * sgu_g.shape[-1]
    bw = conv_w.shape[-1]
    causal = jnp.tril(jnp.ones((CHUNK, CHUNK), dtype=bool))
    ws = jnp.where(causal[None], w_spatial, 0.0).astype(BF16)
    bsp = jnp.repeat(b_spatial.T, aw // A_HEADS, axis=1)
    const = lambda *shape: pl.BlockSpec(shape, lambda b, i: (0,) * len(shape))
    return pl.pallas_call(
        functools.partial(_mixer_kernel, alpha=alpha, ts=ts, aw=aw),
        grid=(bsz, seq // ts),
        in_specs=[
            pl.BlockSpec((1, ts, d), lambda b, i: (b, i, 0)),
            const(d, w_in.shape[1]), const(1, w_in.shape[1]), const(1, aw), const(1, aw),
            const(A_HEADS, CHUNK, CHUNK), const(CHUNK, aw), const(conv_w.shape[0], bw),
            const(aw + bw, d), const(1, d), const(1, d),
        ],
        out_specs=pl.BlockSpec((1, ts, d), lambda b, i: (b, i, 0)),
        out_shape=jax.ShapeDtypeStruct(x.shape, F32),
        scratch_shapes=[pltpu.VMEM((ts + 2 * SUBLANES, bw), F32)],
        compiler_params=pltpu.CompilerParams(
            dimension_semantics=("parallel", "arbitrary"), vmem_limit_bytes=VMEM_LIMIT),
        name="mixer",
    )(x, w_in.astype(BF16), b_in.reshape(1, -1), sgu_g.reshape(1, -1), sgu_b.reshape(1, -1),
      ws, bsp, conv_w.reshape(conv_w.shape[0], bw), w_out.astype(BF16),
      ln_g.reshape(1, -1), ln_b.reshape(1, -1))


def _kv_kernel(m_ref, w_ref, o_ref):
    o_ref[0] = _dot(m_ref[0].astype(BF16), w_ref[...]).astype(BF16)


def _kv_proj(mem, w_kv):
    bsz, m, d = mem.shape
    n = w_kv.shape[1]
    return pl.pallas_call(
        _kv_kernel,
        grid=(bsz,),
        in_specs=[pl.BlockSpec((1, m, d), lambda b: (b, 0, 0)),
                  pl.BlockSpec((d, n), lambda b: (0, 0))],
        out_specs=pl.BlockSpec((1, m, n), lambda b: (b, 0, 0)),
        out_shape=jax.ShapeDtypeStruct((bsz, m, n), BF16),
        compiler_params=pltpu.CompilerParams(
            dimension_semantics=("parallel",), vmem_limit_bytes=VMEM_LIMIT),
        name="kv_proj",
    )(mem, w_kv.astype(BF16))


def _xattn_kernel(x_ref, kv_ref, wq_ref, wo_ref, lg_ref, lb_ref, o_ref, *, alpha):
    x = x_ref[0]
    d = x.shape[-1]
    dk = d // X_HEADS
    q = _dot(x.astype(BF16), wq_ref[...])
    outs = []
    for h in range(X_HEADS):
        qh = q[:, h * dk:(h + 1) * dk].astype(BF16)
        s = _dot_nt(qh, kv_ref[0, :, h * dk:(h + 1) * dk]) * np.float32(dk ** -0.5)
        e = jnp.exp(s - jnp.max(s, axis=-1, keepdims=True))
        p = e / jnp.sum(e, axis=-1, keepdims=True)
        outs.append(_dot(p.astype(BF16), kv_ref[0, :, d + h * dk:d + (h + 1) * dk]))
    o = jnp.concatenate(outs, axis=-1).astype(BF16)
    o_ref[0] = _ln(alpha * x + _dot(o, wo_ref[...]), lg_ref[...], lb_ref[...])


def _xattn(x, kv, w_q, w_o, ln_g, ln_b, alpha, ts):
    bsz, seq, d = x.shape
    m = kv.shape[1]
    return pl.pallas_call(
        functools.partial(_xattn_kernel, alpha=alpha),
        grid=(bsz, seq // ts),
        in_specs=[
            pl.BlockSpec((1, ts, d), lambda b, i: (b, i, 0)),
            pl.BlockSpec((1, m, 2 * d), lambda b, i: (b, 0, 0)),
            pl.BlockSpec((d, d), lambda b, i: (0, 0)),
            pl.BlockSpec((d, d), lambda b, i: (0, 0)),
            pl.BlockSpec((1, d), lambda b, i: (0, 0)),
            pl.BlockSpec((1, d), lambda b, i: (0, 0)),
        ],
        out_specs=pl.BlockSpec((1, ts, d), lambda b, i: (b, i, 0)),
        out_shape=jax.ShapeDtypeStruct(x.shape, F32),
        compiler_params=pltpu.CompilerParams(
            dimension_semantics=("parallel", "parallel"), vmem_limit_bytes=VMEM_LIMIT),
        name="xattn",
    )(x, kv, w_q.astype(BF16), w_o.astype(BF16), ln_g.reshape(1, -1), ln_b.reshape(1, -1))


def _extract_top(vals, tags, payload, rounds):
    tops, picks = [], []
    for _ in range(rounds):
        m = jnp.max(vals, axis=0, keepdims=True)
        first = jnp.min(jnp.where(vals == m, tags, np.float32(2 ** 24)), axis=0, keepdims=True)
        sel = tags == first
        tops.append(m)
        if payload is None:
            picks.append(first)
        else:
            picks.append(jnp.max(jnp.where(sel, payload, -1.0), axis=0, keepdims=True))
        vals = jnp.where(sel, -jnp.inf, vals)
    return jnp.concatenate(tops, axis=0), jnp.concatenate(picks, axis=0)


def _route_kernel(x_ref, wq_ref, keys_ref, code_ref, idx_ref, gate_ref, idx_t, gate_t):
    tt = x_ref.shape[0]
    xb = x_ref[...].astype(BF16)
    row = lax.broadcasted_iota(jnp.int32, (N_KEYS, tt), 0).astype(F32)
    code = code_ref[...]

    def head(h, carry):
        q = _dot(xb, wq_ref[h]).astype(BF16)
        hk = q.shape[1] // 2
        sa = _dot_nt(keys_ref[h, 0], q[:, :hk])
        sb = _dot_nt(keys_ref[h, 1], q[:, hk:])
        va, ia = _extract_top(sa, row, None, TOPK)
        vb, ib = _extract_top(sb, row, None, TOPK)
        ea = ia * np.float32(N_KEYS)
        cand = [va + vb[0:1], ]
        eid = [ea + ib[0:1], ]
        for j in range(1, SUBLANES):
            cand.append(va[0:SUBLANES] + vb[j:j + 1])
            eid.append(ea[0:SUBLANES] + ib[j:j + 1])
        cand.append(va[0:1] + vb[SUBLANES:])
        eid.append(ea[0:1] + ib[SUBLANES:])
        cand = jnp.where(code >= 0, jnp.concatenate(cand, axis=0), -jnp.inf)
        eid = jnp.concatenate(eid, axis=0)
        top, expert = _extract_top(cand, code, eid, TOPK)
        e = jnp.exp(top - top[0:1])
        r0 = pl.multiple_of(h * TOPK, TOPK)
        gate_t[pl.ds(r0, TOPK), :] = e / jnp.sum(e, axis=0, keepdims=True)
        idx_t[pl.ds(r0, TOPK), :] = expert.astype(jnp.int32)
        return carry

    lax.fori_loop(0, PEER_HEADS, head, 0)
    idx_ref[...] = idx_t[...].T
    gate_ref[...] = gate_t[...].T


def _candidate_codes(tt):
    codes = [[i * TOPK for i in range(TOPK)]]
    for j in range(1, SUBLANES):
        codes.append([i * TOPK + j if (i + 1) * (j + 1) <= TOPK else -1 for i in range(SUBLANES)])
    codes.append(list(range(SUBLANES, TOPK)))
    flat = np.asarray([c for blk in codes for c in blk], dtype=np.float32)
    return jnp.asarray(np.broadcast_to(flat[:, None], (flat.shape[0], tt)))


def _peer_route(xt, w_query, sub_keys, tt):
    n_tok, d = xt.shape
    dkey = w_query.shape[1] // PEER_HEADS
    wq = w_query.reshape(d, PEER_HEADS, dkey).transpose(1, 0, 2).astype(BF16)
    hk = PEER_HEADS * TOPK
    code = _candidate_codes(tt)
    return pl.pallas_call(
        _route_kernel,
        grid=(n_tok // tt,),
        in_specs=[
            pl.BlockSpec((tt, d), lambda i: (i, 0)),
            pl.BlockSpec(wq.shape, lambda i: (0, 0, 0)),
            pl.BlockSpec(sub_keys.shape, lambda i: (0, 0, 0, 0)),
            pl.BlockSpec(code.shape, lambda i: (0, 0)),
        ],
        out_specs=[pl.BlockSpec((tt, hk), lambda i: (i, 0)),
                   pl.BlockSpec((tt, hk), lambda i: (i, 0))],
        out_shape=[jax.ShapeDtypeStruct((n_tok, hk), jnp.int32),
                   jax.ShapeDtypeStruct((n_tok, hk), F32)],
        scratch_shapes=[pltpu.VMEM((hk, tt), jnp.int32), pltpu.VMEM((hk, tt), F32)],
        compiler_params=pltpu.CompilerParams(
            dimension_semantics=("parallel",), vmem_limit_bytes=VMEM_LIMIT),
        name="peer_route",
    )(xt, wq, sub_keys.astype(BF16), code)


ROW_WORDS = 4
TOK_GROUP = 32
N_STAGE = 4


def _pack_rows(w):
    n, d = w.shape
    assert d == 2 * ROW_WORDS * LANES
    bits = lax.bitcast_convert_type(w.astype(BF16), jnp.uint16).astype(jnp.uint32)
    bits = bits.reshape(n, ROW_WORDS, 2, LANES)
    return ((bits[:, :, 1, :] << 16) | bits[:, :, 0, :]).reshape(n * ROW_WORDS, LANES)


def _gather_rows(idx_ref, t, tab_ref, stage_ref, hk):
    for k in range(hk):
        e = pl.multiple_of(idx_ref[t, k], ROW_WORDS)
        stage_ref[k * ROW_WORDS:(k + 1) * ROW_WORDS, :] = tab_ref[pl.ds(e, ROW_WORDS), :]


def _diag_mask(hk):
    r = lax.broadcasted_iota(jnp.int32, (SUBLANES, hk * SUBLANES), 0)
    c = lax.broadcasted_iota(jnp.int32, (SUBLANES, hk * SUBLANES), 1)
    return (c & (SUBLANES - 1)) == r


def _down_kernel(idx_ref, x_ref, gate_ref, fold_ref, tab_ref, c_ref, *stage):
    hk = gate_ref.shape[1]
    mask = _diag_mask(hk)

    def group(g, carry):
        t0 = pl.multiple_of(g * TOK_GROUP, TOK_GROUP)
        parts = []
        for j in range(TOK_GROUP):
            buf = stage[j % N_STAGE]
            _gather_rows(idx_ref, t0 + j, tab_ref, buf, hk)
            rows = pltpu.bitcast(buf[...], BF16)
            s = _dot_nt(x_ref[t0 + j], rows)
            parts.append(jnp.where(mask, s, 0.0).astype(BF16))
        z = _dot(jnp.concatenate(parts, axis=0), fold_ref[...])
        act = jnp.concatenate(
            [jnp.sum(z[j * SUBLANES:(j + 1) * SUBLANES], axis=0, keepdims=True)
             for j in range(TOK_GROUP)], axis=0)
        c_ref[pl.ds(t0, TOK_GROUP), :] = gate_ref[pl.ds(t0, TOK_GROUP), :] * _gelu(act)
        return carry

    lax.fori_loop(0, x_ref.shape[0] // TOK_GROUP, group, 0)


def _stage_scratch(hk):
    return [pltpu.VMEM((hk * ROW_WORDS, LANES), jnp.uint32) for _ in range(N_STAGE)]


def _peer_down(idx, x3, gate, tab, tb):
    n_tok, hk = idx.shape
    fold = (jnp.arange(hk * SUBLANES)[:, None] // SUBLANES == jnp.arange(hk)[None, :]).astype(BF16)
    return pl.pallas_call(
        _down_kernel,
        grid=(n_tok // tb,),
        in_specs=[
            pl.BlockSpec((tb, hk), lambda i: (i, 0), memory_space=pltpu.SMEM),
            pl.BlockSpec((tb, SUBLANES, LANES), lambda i: (i, 0, 0)),
            pl.BlockSpec((tb, hk), lambda i: (i, 0)),
            pl.BlockSpec(fold.shape, lambda i: (0, 0)),
            pl.BlockSpec(memory_space=pltpu.VMEM),
        ],
        out_specs=pl.BlockSpec((tb, hk), lambda i: (i, 0)),
        out_shape=jax.ShapeDtypeStruct((n_tok, hk), F32),
        scratch_shapes=_stage_scratch(hk),
        compiler_params=pltpu.CompilerParams(
            dimension_semantics=("arbitrary",), vmem_limit_bytes=VMEM_LIMIT),
        name="peer_down",
    )(idx, x3, gate, fold, tab)


def _up_kernel(idx_ref, c_ref, spread_ref, tab_ref, o_ref, *stage):
    hk = c_ref.shape[1]
    mask = _diag_mask(hk)

    def group(g, carry):
        t0 = pl.multiple_of(g * TOK_GROUP, TOK_GROUP)
        cexp = _dot(c_ref[pl.ds(t0, TOK_GROUP), :].astype(BF16), spread_ref[...])
        for j in range(TOK_GROUP):
            buf = stage[j % N_STAGE]
            _gather_rows(idx_ref, t0 + j, tab_ref, buf, hk)
            rows = pltpu.bitcast(buf[...], BF16)
            cj = jnp.where(mask, jnp.broadcast_to(cexp[j:j + 1, :], mask.shape), 0.0)
            o_ref[t0 + j] = _dot(cj.astype(BF16), rows)
        return carry

    lax.fori_loop(0, c_ref.shape[0] // TOK_GROUP, group, 0)


def _peer_up(idx, c, tab, tb):
    n_tok, hk = idx.shape
    spread = (jnp.arange(hk)[:, None] == jnp.arange(hk * SUBLANES)[None, :] // SUBLANES).astype(BF16)
    return pl.pallas_call(
        _up_kernel,
        grid=(n_tok // tb,),
        in_specs=[
            pl.BlockSpec((tb, hk), lambda i: (i, 0), memory_space=pltpu.SMEM),
            pl.BlockSpec((tb, hk), lambda i: (i, 0)),
            pl.BlockSpec(spread.shape, lambda i: (0, 0)),
            pl.BlockSpec(memory_space=pltpu.VMEM),
        ],
        out_specs=pl.BlockSpec((tb, SUBLANES, LANES), lambda i: (i, 0, 0)),
        out_shape=jax.ShapeDtypeStruct((n_tok, SUBLANES, LANES), F32),
        scratch_shapes=_stage_scratch(hk),
        compiler_params=pltpu.CompilerParams(
            dimension_semantics=("arbitrary",), vmem_limit_bytes=VMEM_LIMIT),
        name="peer_up",
    )(idx, c, spread, tab)


def _add_ln_kernel(x_ref, h_ref, g_ref, b_ref, o_ref, *, alpha):
    o_ref[...] = _ln(alpha * x_ref[...] + h_ref[...], g_ref[...], b_ref[...])


def _add_ln(x, h, g, b, alpha, tt):
    n_tok, d = x.shape
    return pl.pallas_call(
        functools.partial(_add_ln_kernel, alpha=alpha),
        grid=(n_tok // tt,),
        in_specs=[pl.BlockSpec((tt, d), lambda i: (i, 0)), pl.BlockSpec((tt, d), lambda i: (i, 0)),
                  pl.BlockSpec((1, d), lambda i: (0, 0)), pl.BlockSpec((1, d), lambda i: (0, 0))],
        out_specs=pl.BlockSpec((tt, d), lambda i: (i, 0)),
        out_shape=jax.ShapeDtypeStruct(x.shape, F32),
        compiler_params=pltpu.CompilerParams(
            dimension_semantics=("parallel",), vmem_limit_bytes=VMEM_LIMIT),
        name="add_ln",
    )(x, h, g.reshape(1, -1), b.reshape(1, -1))


def _tile(n, want):
    return want if n % want == 0 else n


def kernel(x, mem, w_in, b_in, sgu_g, sgu_b, w_spatial, b_spatial, conv_w, w_mix_out, ln1_g, ln1_b,
           w_xq, w_xkv, w_xo, ln2_g, ln2_b, w_pq, sub_keys, w_down, w_up, ln3_g, ln3_b):
    depth = w_in.shape[0]
    bsz, seq, d = x.shape
    assert d == SUBLANES * LANES and seq % CHUNK == 0
    alpha = np.float32((2 * depth) ** 0.25)
    n_tok = bsz * seq
    ts = _tile(seq, 512)
    for l in range(depth):
        x = _mixer(x, w_in[l], b_in[l], sgu_g[l], sgu_b[l], w_spatial[l], b_spatial[l], conv_w[l],
                   w_mix_out[l], ln1_g[l], ln1_b[l], alpha, ts)
        kv = _kv_proj(mem, w_xkv[l])
        x = _xattn(x, kv, w_xq[l], w_xo[l], ln2_g[l], ln2_b[l], alpha, ts)
        xt = x.reshape(n_tok, d)
        idx, gate = _peer_route(xt, w_pq[l], sub_keys[l], _tile(n_tok, 256))
        idx = idx * ROW_WORDS
        tb = _tile(n_tok, 128)
        x3 = xt.reshape(n_tok, SUBLANES, LANES).astype(BF16)
        c = _peer_down(idx, x3, gate, _pack_rows(w_down[l]), tb)
        h = _peer_up(idx, c, _pack_rows(w_up[l]), tb)
        x = _add_ln(xt, h.reshape(n_tok, d), ln3_g[l], ln3_b[l], alpha, _tile(n_tok, 512))
        x = x.reshape(bsz, seq, d)
    return x
```

```python
import functools

import jax
import jax.numpy as jnp
import numpy as np
from jax import lax
from jax.experimental import pallas as pl
from jax.experimental.pallas import tpu as pltpu

LN_EPS = 1e-5
LANES = 128
SUBLANES = 8
A_HEADS = 4
CHUNK = 128
X_HEADS = 4
PEER_HEADS = 8
N_KEYS = 128
TOPK = 16
VMEM_LIMIT = 56 * 1024 * 1024

F32 = jnp.float32
BF16 = jnp.bfloat16


def _gelu(x):
    return 0.5 * x * (1.0 + lax.erf(x * np.float32(2.0 ** -0.5)))


def _ln(v, g, b):
    mu = jnp.mean(v, axis=-1, keepdims=True)
    d = v - mu
    var = jnp.mean(d * d, axis=-1, keepdims=True)
    return d * lax.rsqrt(var + LN_EPS) * g + b


def _dot(a, b):
    return jnp.dot(a, b, preferred_element_type=F32)


def _dot_nt(a, b):
    return lax.dot_general(a, b, (((1,), (1,)), ((), ())), preferred_element_type=F32)


def _mixer_kernel(x_ref, win_ref, bin_ref, sg_ref, sb_ref, ws_ref, bsp_ref, cw_ref, wout_ref,
                  lg_ref, lb_ref, o_ref, zc_ref, *, alpha, ts, aw):
    @pl.when(pl.program_id(1) == 0)
    def _():
        zc_ref[0:SUBLANES, :] = jnp.zeros((SUBLANES, zc_ref.shape[1]), F32)

    x = x_ref[0]
    xb = x.astype(BF16)

    def zcol(lo, hi):
        return _dot(xb, win_ref[:, lo:hi]) + bin_ref[:, lo:hi]

    hd = aw // A_HEADS
    u = _gelu(zcol(0, aw))
    v = _gelu(zcol(aw, 2 * aw))
    mixed = []
    for h in range(A_HEADS):
        sl = slice(h * hd, (h + 1) * hd)
        vh = _ln(v[:, sl], sg_ref[:, sl], sb_ref[:, sl]).astype(BF16)
        mixed.append(jnp.concatenate(
            [_dot(ws_ref[h], vh[c * CHUNK:(c + 1) * CHUNK]) + bsp_ref[:, sl]
             for c in range(ts // CHUNK)], axis=0))
    ya = u * jnp.concatenate(mixed, axis=-1)

    bw = zc_ref.shape[1]
    hb = zcol(2 * aw, 2 * aw + bw)
    gb = zcol(2 * aw + bw, 2 * aw + 2 * bw)
    gc = zcol(2 * aw + 2 * bw, 2 * aw + 3 * bw)
    zc_ref[SUBLANES:SUBLANES + ts, :] = gc * hb
    conv = (cw_ref[0:1, :] * zc_ref[SUBLANES - 2:SUBLANES - 2 + ts, :]
            + cw_ref[1:2, :] * zc_ref[SUBLANES - 1:SUBLANES - 1 + ts, :]
            + cw_ref[2:3, :] * zc_ref[SUBLANES:SUBLANES + ts, :])
    yb = gb * conv
    zc_ref[0:SUBLANES, :] = zc_ref[ts:ts + SUBLANES, :]

    h = _dot(ya.astype(BF16), wout_ref[0:aw, :]) + _dot(yb.astype(BF16), wout_ref[aw:, :])
    o_ref[0] = _ln(alpha * x + h, lg_ref[...], lb_ref[...])


def _mixer(x, w_in, b_in, sgu_g, sgu_b, w_spatial, b_spatial, conv_w, w_out, ln_g, ln_b, alpha, ts):
    bsz, seq, d = x.shape
    aw = A_HEADS * sgu_g.shape[-1]
    bw = conv_w.shape[-1]
    causal = jnp.tril(jnp.ones((CHUNK, CHUNK), dtype=bool))
    ws = jnp.where(causal[None], w_spatial, 0.0).astype(BF16)
    bsp = jnp.repeat(b_spatial.T, aw // A_HEADS, axis=1)
    const = lambda *shape: pl.BlockSpec(shape, lambda b, i: (0,) * len(shape))
    return pl.pallas_call(
        functools.partial(_mixer_kernel, alpha=alpha, ts=ts, aw=aw),
        grid=(bsz, seq // ts),
        in_specs=[
            pl.BlockSpec((1, ts, d), lambda b, i: (b, i, 0)),
            const(d, w_in.shape[1]), const(1, w_in.shape[1]), const(1, aw), const(1, aw),
            const(A_HEADS, CHUNK, CHUNK), const(CHUNK, aw), const(conv_w.shape[0], bw),
            const(aw + bw, d), const(1, d), const(1, d),
        ],
        out_specs=pl.BlockSpec((1, ts, d), lambda b, i: (b, i, 0)),
        out_shape=jax.ShapeDtypeStruct(x.shape, F32),
        scratch_shapes=[pltpu.VMEM((ts + 2 * SUBLANES, bw), F32)],
        compiler_params=pltpu.CompilerParams(
            dimension_semantics=("parallel", "arbitrary"), vmem_limit_bytes=VMEM_LIMIT),
        name="mixer",
    )(x, w_in.astype(BF16), b_in.reshape(1, -1), sgu_g.reshape(1, -1), sgu_b.reshape(1, -1),
      ws, bsp, conv_w.reshape(conv_w.shape[0], bw), w_out.astype(BF16),
      ln_g.reshape(1, -1), ln_b.reshape(1, -1))


def _kv_kernel(m_ref, w_ref, o_ref):
    o_ref[0] = _dot(m_ref[0].astype(BF16), w_ref[...]).astype(BF16)


def _kv_proj(mem, w_kv):
    bsz, m, d = mem.shape
    n = w_kv.shape[1]
    return pl.pallas_call(
        _kv_kernel,
        grid=(bsz,),
        in_specs=[pl.BlockSpec((1, m, d), lambda b: (b, 0, 0)),
                  pl.BlockSpec((d, n), lambda b: (0, 0))],
        out_specs=pl.BlockSpec((1, m, n), lambda b: (b, 0, 0)),
        out_shape=jax.ShapeDtypeStruct((bsz, m, n), BF16),
        compiler_params=pltpu.CompilerParams(
            dimension_semantics=("parallel",), vmem_limit_bytes=VMEM_LIMIT),
        name="kv_proj",
    )(mem, w_kv.astype(BF16))


def _xattn_kernel(x_ref, kv_ref, wq_ref, wo_ref, lg_ref, lb_ref, o_ref, *, alpha):
    x = x_ref[0]
    d = x.shape[-1]
    dk = d // X_HEADS
    q = _dot(x.astype(BF16), wq_ref[...])
    outs = []
    for h in range(X_HEADS):
        qh = q[:, h * dk:(h + 1) * dk].astype(BF16)
        s = _dot_nt(qh, kv_ref[0, :, h * dk:(h + 1) * dk]) * np.float32(dk ** -0.5)
        e = jnp.exp(s - jnp.max(s, axis=-1, keepdims=True))
        p = e / jnp.sum(e, axis=-1, keepdims=True)
        outs.append(_dot(p.astype(BF16), kv_ref[0, :, d + h * dk:d + (h + 1) * dk]))
    o = jnp.concatenate(outs, axis=-1).astype(BF16)
    o_ref[0] = _ln(alpha * x + _dot(o, wo_ref[...]), lg_ref[...], lb_ref[...])


def _xattn(x, kv, w_q, w_o, ln_g, ln_b, alpha, ts):
    bsz, seq, d = x.shape
    m = kv.shape[1]
    return pl.pallas_call(
        functools.partial(_xattn_kernel, alpha=alpha),
        grid=(bsz, seq // ts),
        in_specs=[
            pl.BlockSpec((1, ts, d), lambda b, i: (b, i, 0)),
            pl.BlockSpec((1, m, 2 * d), lambda b, i: (b, 0, 0)),
            pl.BlockSpec((d, d), lambda b, i: (0, 0)),
            pl.BlockSpec((d, d), lambda b, i: (0, 0)),
            pl.BlockSpec((1, d), lambda b, i: (0, 0)),
            pl.BlockSpec((1, d), lambda b, i: (0, 0)),
        ],
        out_specs=pl.BlockSpec((1, ts, d), lambda b, i: (b, i, 0)),
        out_shape=jax.ShapeDtypeStruct(x.shape, F32),
        compiler_params=pltpu.CompilerParams(
            dimension_semantics=("parallel", "parallel"), vmem_limit_bytes=VMEM_LIMIT),
        name="xattn",
    )(x, kv, w_q.astype(BF16), w_o.astype(BF16), ln_g.reshape(1, -1), ln_b.reshape(1, -1))


def _extract_top(vals, tags, payload, rounds):
    tops, picks = [], []
    for _ in range(rounds):
        m = jnp.max(vals, axis=0, keepdims=True)
        first = jnp.min(jnp.where(vals == m, tags, np.float32(2 ** 24)), axis=0, keepdims=True)
        sel = tags == first
        tops.append(m)
        if payload is None:
            picks.append(first)
        else:
            picks.append(jnp.max(jnp.where(sel, payload, -1.0), axis=0, keepdims=True))
        vals = jnp.where(sel, -jnp.inf, vals)
    return jnp.concatenate(tops, axis=0), jnp.concatenate(picks, axis=0)


def _route_kernel(x_ref, wq_ref, keys_ref, code_ref, idx_ref, gate_ref, idx_t, gate_t):
    tt = x_ref.shape[0]
    xb = x_ref[...].astype(BF16)
    row = lax.broadcasted_iota(jnp.int32, (N_KEYS, tt), 0).astype(F32)
    code = code_ref[...]

    def head(h, carry):
        q = _dot(xb, wq_ref[h]).astype(BF16)
        hk = q.shape[1] // 2
        sa = _dot_nt(keys_ref[h, 0], q[:, :hk])
        sb = _dot_nt(keys_ref[h, 1], q[:, hk:])
        va, ia = _extract_top(sa, row, None, TOPK)
        vb, ib = _extract_top(sb, row, None, TOPK)
        ea = ia * np.float32(N_KEYS)
        cand = [va + vb[0:1], ]
        eid = [ea + ib[0:1], ]
        for j in range(1, SUBLANES):
            cand.append(va[0:SUBLANES] + vb[j:j + 1])
            eid.append(ea[0:SUBLANES] + ib[j:j + 1])
        cand.append(va[0:1] + vb[SUBLANES:])
        eid.append(ea[0:1] + ib[SUBLANES:])
        cand = jnp.where(code >= 0, jnp.concatenate(cand, axis=0), -jnp.inf)
        eid = jnp.concatenate(eid, axis=0)
        top, expert = _extract_top(cand, code, eid, TOPK)
        e = jnp.exp(top - top[0:1])
        r0 = pl.multiple_of(h * TOPK, TOPK)
        gate_t[pl.ds(r0, TOPK), :] = e / jnp.sum(e, axis=0, keepdims=True)
        idx_t[pl.ds(r0, TOPK), :] = expert.astype(jnp.int32)
        return carry

    lax.fori_loop(0, PEER_HEADS, head, 0)
    idx_ref[...] = idx_t[...].T
    gate_ref[...] = gate_t[...].T


def _candidate_codes(tt):
    codes = [[i * TOPK for i in range(TOPK)]]
    for j in range(1, SUBLANES):
        codes.append([i * TOPK + j if (i + 1) * (j + 1) <= TOPK else -1 for i in range(SUBLANES)])
    codes.append(list(range(SUBLANES, TOPK)))
    flat = np.asarray([c for blk in codes for c in blk], dtype=np.float32)
    return jnp.asarray(np.broadcast_to(flat[:, None], (flat.shape[0], tt)))


def _peer_route(xt, w_query, sub_keys, tt):
    n_tok, d = xt.shape
    dkey = w_query.shape[1] // PEER_HEADS
    wq = w_query.reshape(d, PEER_HEADS, dkey).transpose(1, 0, 2).astype(BF16)
    hk = PEER_HEADS * TOPK
    code = _candidate_codes(tt)
    return pl.pallas_call(
        _route_kernel,
        grid=(n_tok // tt,),
        in_specs=[
            pl.BlockSpec((tt, d), lambda i: (i, 0)),
            pl.BlockSpec(wq.shape, lambda i: (0, 0, 0)),
            pl.BlockSpec(sub_keys.shape, lambda i: (0, 0, 0, 0)),
            pl.BlockSpec(code.shape, lambda i: (0, 0)),
        ],
        out_specs=[pl.BlockSpec((tt, hk), lambda i: (i, 0)),
                   pl.BlockSpec((tt, hk), lambda i: (i, 0))],
        out_shape=[jax.ShapeDtypeStruct((n_tok, hk), jnp.int32),
                   jax.ShapeDtypeStruct((n_tok, hk), F32)],
        scratch_shapes=[pltpu.VMEM((hk, tt), jnp.int32), pltpu.VMEM((hk, tt), F32)],
        compiler_params=pltpu.CompilerParams(
            dimension_semantics=("parallel",), vmem_limit_bytes=VMEM_LIMIT),
        name="peer_route",
    )(xt, wq, sub_keys.astype(BF16), code)


ROW_WORDS = 4
TOK_GROUP = 32


def _pack_rows(w):
    n, d = w.shape
    assert d == 2 * ROW_WORDS * LANES
    bits = lax.bitcast_convert_type(w.astype(BF16), jnp.uint16).astype(jnp.uint32)
    bits = bits.reshape(n, ROW_WORDS, 2, LANES)
    return ((bits[:, :, 1, :] << 16) | bits[:, :, 0, :]).reshape(n * ROW_WORDS, LANES)


def _gather_rows(idx_ref, t, tab_ref, hk):
    words = [tab_ref[pl.ds(pl.multiple_of(idx_ref[t, k], ROW_WORDS), ROW_WORDS), :] for k in range(hk)]
    return pltpu.bitcast(jnp.concatenate(words, axis=0), BF16)


def _diag_mask(hk):
    r = lax.broadcasted_iota(jnp.int32, (SUBLANES, hk * SUBLANES), 0)
    c = lax.broadcasted_iota(jnp.int32, (SUBLANES, hk * SUBLANES), 1)
    return (c & (SUBLANES - 1)) == r


def _down_kernel(idx_ref, x_ref, gate_ref, fold_ref, tab_ref, c_ref):
    hk = gate_ref.shape[1]
    mask = _diag_mask(hk)

    def group(g, carry):
        t0 = pl.multiple_of(g * TOK_GROUP, TOK_GROUP)
        parts = []
        for j in range(TOK_GROUP):
            rows = _gather_rows(idx_ref, t0 + j, tab_ref, hk)
            s = _dot_nt(x_ref[t0 + j], rows)
            parts.append(jnp.where(mask, s, 0.0).astype(BF16))
        z = _dot(jnp.concatenate(parts, axis=0), fold_ref[...])
        act = jnp.concatenate(
            [jnp.sum(z[j * SUBLANES:(j + 1) * SUBLANES], axis=0, keepdims=True)
             for j in range(TOK_GROUP)], axis=0)
        c_ref[pl.ds(t0, TOK_GROUP), :] = gate_ref[pl.ds(t0, TOK_GROUP), :] * _gelu(act)
        return carry

    lax.fori_loop(0, x_ref.shape[0] // TOK_GROUP, group, 0)


def _peer_down(idx, x3, gate, tab, tb):
    n_tok, hk = idx.shape
    fold = (jnp.arange(hk * SUBLANES)[:, None] // SUBLANES == jnp.arange(hk)[None, :]).astype(BF16)
    return pl.pallas_call(
        _down_kernel,
        grid=(n_tok // tb,),
        in_specs=[
            pl.BlockSpec((tb, hk), lambda i: (i, 0), memory_space=pltpu.SMEM),
            pl.BlockSpec((tb, SUBLANES, LANES), lambda i: (i, 0, 0)),
            pl.BlockSpec((tb, hk), lambda i: (i, 0)),
            pl.BlockSpec(fold.shape, lambda i: (0, 0)),
            pl.BlockSpec(memory_space=pltpu.VMEM),
        ],
        out_specs=pl.BlockSpec((tb, hk), lambda i: (i, 0)),
        out_shape=jax.ShapeDtypeStruct((n_tok, hk), F32),
        compiler_params=pltpu.CompilerParams(
            dimension_semantics=("arbitrary",), vmem_limit_bytes=VMEM_LIMIT),
        name="peer_down",
    )(idx, x3, gate, fold, tab)


def _up_kernel(idx_ref, c_ref, spread_ref, tab_ref, o_ref):
    hk = c_ref.shape[1]
    mask = _diag_mask(hk)

    def group(g, carry):
        t0 = pl.multiple_of(g * TOK_GROUP, TOK_GROUP)
        cexp = _dot(c_ref[pl.ds(t0, TOK_GROUP), :].astype(BF16), spread_ref[...])
        for j in range(TOK_GROUP):
            rows = _gather_rows(idx_ref, t0 + j, tab_ref, hk)
            cj =jnp.where(mask, jnp.broadcast_to(cexp[j:j + 1, :], mask.shape), 0.0)
            o_ref[t0 + j] = _dot(cj.astype(BF16), rows)
        return carry

    lax.fori_loop(0, c_ref.shape[0] // TOK_GROUP, group, 0)


def _peer_up(idx, c, tab, tb):
    n_tok, hk = idx.shape
    spread = (jnp.arange(hk)[:, None] == jnp.arange(hk * SUBLANES)[None, :] // SUBLANES).astype(BF16)
    return pl.pallas_call(
        _up_kernel,
        grid=(n_tok // tb,),
        in_specs=[
            pl.BlockSpec((tb, hk), lambda i: (i, 0), memory_space=pltpu.SMEM),
            pl.BlockSpec((tb, hk), lambda i: (i, 0)),
            pl.BlockSpec(spread.shape, lambda i: (0, 0)),
            pl.BlockSpec(memory_space=pltpu.VMEM),
        ],
        out_specs=pl.BlockSpec((tb, SUBLANES, LANES), lambda i: (i, 0, 0)),
        out_shape=jax.ShapeDtypeStruct((n_tok, SUBLANES, LANES), F32),
        compiler_params=pltpu.CompilerParams(
            dimension_semantics=("arbitrary",), vmem_limit_bytes=VMEM_LIMIT),
        name="peer_up",
    )(idx, c, spread, tab)


def _add_ln_kernel(x_ref, h_ref, g_ref, b_ref, o_ref, *, alpha):
    o_ref[...] = _ln(alpha * x_ref[...] + h_ref[...], g_ref[...], b_ref[...])


def _add_ln(x, h, g, b, alpha, tt):
    n_tok, d = x.shape
    return pl.pallas_call(
        functools.partial(_add_ln_kernel, alpha=alpha),
        grid=(n_tok // tt,),
        in_specs=[pl.BlockSpec((tt, d), lambda i: (i, 0)), pl.BlockSpec((tt, d), lambda i: (i, 0)),
                  pl.BlockSpec((1, d), lambda i: (0, 0)), pl.BlockSpec((1, d), lambda i: (0, 0))],
        out_specs=pl.BlockSpec((tt, d), lambda i: (i, 0)),
        out_shape=jax.ShapeDtypeStruct(x.shape, F32),
        compiler_params=pltpu.CompilerParams(
            dimension_semantics=("parallel",), vmem_limit_bytes=VMEM_LIMIT),
        name="add_ln",
    )(x, h, g.reshape(1, -1), b.reshape(1, -1))


def _tile(n, want):
    return want if n % want == 0 else n


def kernel(x, mem, w_in, b_in, sgu_g, sgu_b, w_spatial, b_spatial, conv_w, w_mix_out, ln1_g, ln1_b,
           w_xq, w_xkv, w_xo, ln2_g, ln2_b, w_pq, sub_keys, w_down, w_up, ln3_g, ln3_b):
    depth = w_in.shape[0]
    bsz, seq, d = x.shape
    assert d == SUBLANES * LANES and seq % CHUNK == 0
    alpha = np.float32((2 * depth) ** 0.25)
    n_tok = bsz * seq
    ts = _tile(seq, 512)
    for l in range(depth):
        x = _mixer(x, w_in[l], b_in[l], sgu_g[l], sgu_b[l], w_spatial[l], b_spatial[l], conv_w[l],
                   w_mix_out[l], ln1_g[l], ln1_b[l], alpha, ts)
        kv = _kv_proj(mem, w_xkv[l])
        x = _xattn(x, kv, w_xq[l], w_xo[l], ln2_g[l], ln2_b[l], alpha, ts)
        xt = x.reshape(n_tok, d)
        idx, gate = _peer_route(xt, w_pq[l], sub_keys[l], _tile(n_tok, 256))
        idx = idx * ROW_WORDS
        tb = _tile(n_tok, 128)
        x3 = xt.reshape(n_tok, SUBLANES, LANES).astype(BF16)
        c = _peer_down(idx, x3, gate, _pack_rows(w_down[l]), tb)
        h = _peer_up(idx, c, _pack_rows(w_up[l]), tb)
        x = _add_ln(xt, h.reshape(n_tok, d), ln3_g[l], ln3_b[l], alpha, _tile(n_tok, 512))
        x = x.reshape(bsz, seq, d)
    return x
```

```python
import functools

import jax
import jax.numpy as jnp
import numpy as np
from jax import lax
from jax.experimental import pallas as pl
from jax.experimental.pallas import tpu as pltpu

LN_EPS = 1e-5
LANES = 128
SUBLANES = 8
A_HEADS = 4
CHUNK = 128
X_HEADS = 4
PEER_HEADS = 8
N_KEYS = 128
TOPK = 16
VMEM_LIMIT = 56 * 1024 * 1024

F32 = jnp.float32
BF16 = jnp.bfloat16


def _gelu(x):
    return 0.5 * x * (1.0 + lax.erf(x * np.float32(2.0 ** -0.5)))


def _ln(v, g, b):
    mu = jnp.mean(v, axis=-1, keepdims=True)
    d = v - mu
    var = jnp.mean(d * d, axis=-1, keepdims=True)
    return d * lax.rsqrt(var + LN_EPS) * g + b


def _dot(a, b):
    return jnp.dot(a, b, preferred_element_type=F32)


def _dot_nt(a, b):
    return lax.dot_general(a, b, (((1,), (1,)), ((), ())), preferred_element_type=F32)


def _mixer_kernel(x_ref, win_ref, bin_ref, sg_ref, sb_ref, ws_ref, bsp_ref, cw_ref, wout_ref,
                  lg_ref, lb_ref, o_ref, zc_ref, *, alpha, ts, aw):
    @pl.when(pl.program_id(1) == 0)
    def _():
        zc_ref[0:SUBLANES, :] = jnp.zeros((SUBLANES, zc_ref.shape[1]), F32)

    x = x_ref[0]
    xb = x.astype(BF16)

    def zcol(lo, hi):
        return _dot(xb, win_ref[:, lo:hi]) + bin_ref[:, lo:hi]

    hd = aw // A_HEADS
    u = _gelu(zcol(0, aw))
    v = _gelu(zcol(aw, 2 * aw))
    mixed = []
    for h in range(A_HEADS):
        sl = slice(h * hd, (h + 1) * hd)
        vh = _ln(v[:, sl], sg_ref[:, sl], sb_ref[:, sl]).astype(BF16)
        mixed.append(jnp.concatenate(
            [_dot(ws_ref[h], vh[c * CHUNK:(c + 1) * CHUNK]) + bsp_ref[:, sl]
             for c in range(ts // CHUNK)], axis=0))
    ya = u * jnp.concatenate(mixed, axis=-1)

    bw = zc_ref.shape[1]
    hb = zcol(2 * aw, 2 * aw + bw)
    gb = zcol(2 * aw + bw, 2 * aw + 2 * bw)
    gc = zcol(2 * aw + 2 * bw, 2 * aw + 3 * bw)
    zc_ref[SUBLANES:SUBLANES + ts, :] = gc * hb
    conv = (cw_ref[0:1, :] * zc_ref[SUBLANES - 2:SUBLANES - 2 + ts, :]
            + cw_ref[1:2, :] * zc_ref[SUBLANES - 1:SUBLANES - 1 + ts, :]
            + cw_ref[2:3, :] * zc_ref[SUBLANES:SUBLANES + ts, :])
    yb = gb * conv
    zc_ref[0:SUBLANES, :] = zc_ref[ts:ts + SUBLANES, :]

    h = _dot(ya.astype(BF16), wout_ref[0:aw, :]) + _dot(yb.astype(BF16), wout_ref[aw:, :])
    o_ref[0] = _ln(alpha * x + h, lg_ref[...], lb_ref[...])


def _mixer(x, w_in, b_in, sgu_g, sgu_b, w_spatial, b_spatial, conv_w, w_out, ln_g, ln_b, alpha, ts):
    bsz, seq, d = x.shape
    aw = A_HEADS * sgu_g.shape[-1]
    bw = conv_w.shape[-1]
    causal = jnp.tril(jnp.ones((CHUNK, CHUNK), dtype=bool))
    ws = jnp.where(causal[None], w_spatial, 0.0).astype(BF16)
    bsp = jnp.repeat(b_spatial.T, aw // A_HEADS, axis=1)
    const = lambda *shape: pl.BlockSpec(shape, lambda b, i: (0,) * len(shape))
    return pl.pallas_call(
        functools.partial(_mixer_kernel, alpha=alpha, ts=ts, aw=aw),
        grid=(bsz, seq // ts),
        in_specs=[
            pl.BlockSpec((1, ts, d), lambda b, i: (b, i, 0)),
            const(d, w_in.shape[1]), const(1, w_in.shape[1]), const(1, aw), const(1, aw),
            const(A_HEADS, CHUNK, CHUNK), const(CHUNK, aw), const(conv_w.shape[0], bw),
            const(aw + bw, d), const(1, d), const(1, d),
        ],
        out_specs=pl.BlockSpec((1, ts, d), lambda b, i: (b, i, 0)),
        out_shape=jax.ShapeDtypeStruct(x.shape, F32),
        scratch_shapes=[pltpu.VMEM((ts + 2 * SUBLANES, bw), F32)],
        compiler_params=pltpu.CompilerParams(
            dimension_semantics=("parallel", "arbitrary"), vmem_limit_bytes=VMEM_LIMIT),
        name="mixer",
    )(x, w_in.astype(BF16), b_in.reshape(1, -1), sgu_g.reshape(1, -1), sgu_b.reshape(1, -1),
      ws, bsp, conv_w.reshape(conv_w.shape[0], bw), w_out.astype(BF16),
      ln_g.reshape(1, -1), ln_b.reshape(1, -1))


def _kv_kernel(m_ref, w_ref, o_ref):
    o_ref[0] = _dot(m_ref[0].astype(BF16), w_ref[...]).astype(BF16)


def _kv_proj(mem, w_kv):
    bsz, m, d = mem.shape
    n = w_kv.shape[1]
    return pl.pallas_call(
        _kv_kernel,
        grid=(bsz,),
        in_specs=[pl.BlockSpec((1, m, d), lambda b: (b, 0, 0)),
                  pl.BlockSpec((d, n), lambda b: (0, 0))],
        out_specs=pl.BlockSpec((1, m, n), lambda b: (b, 0, 0)),
        out_shape=jax.ShapeDtypeStruct((bsz, m, n), BF16),
        compiler_params=pltpu.CompilerParams(
            dimension_semantics=("parallel",), vmem_limit_bytes=VMEM_LIMIT),
        name="kv_proj",
    )(mem, w_kv.astype(BF16))


def _xattn_kernel(x_ref, kv_ref, wq_ref, wo_ref, lg_ref, lb_ref, o_ref, *, alpha):
    x = x_ref[0]
    d = x.shape[-1]
    dk = d // X_HEADS
    q = _dot(x.astype(BF16), wq_ref[...])
    outs = []
    for h in range(X_HEADS):
        qh = q[:, h * dk:(h + 1) * dk].astype(BF16)
        s = _dot_nt(qh, kv_ref[0, :, h * dk:(h + 1) * dk]) * np.float32(dk ** -0.5)
        e = jnp.exp(s - jnp.max(s, axis=-1, keepdims=True))
        p = e / jnp.sum(e, axis=-1, keepdims=True)
        outs.append(_dot(p.astype(BF16), kv_ref[0, :, d + h * dk:d + (h + 1) * dk]))
    o = jnp.concatenate(outs, axis=-1).astype(BF16)
    o_ref[0] = _ln(alpha * x + _dot(o, wo_ref[...]), lg_ref[...], lb_ref[...])


def _xattn(x, kv, w_q, w_o, ln_g, ln_b, alpha, ts):
    bsz, seq, d = x.shape
    m = kv.shape[1]
    return pl.pallas_call(
        functools.partial(_xattn_kernel, alpha=alpha),
        grid=(bsz, seq // ts),
        in_specs=[
            pl.BlockSpec((1, ts, d), lambda b, i: (b, i, 0)),
            pl.BlockSpec((1, m, 2 * d), lambda b, i: (b, 0, 0)),
            pl.BlockSpec((d, d), lambda b, i: (0, 0)),
            pl.BlockSpec((d, d), lambda b, i: (0, 0)),
            pl.BlockSpec((1, d), lambda b, i: (0, 0)),
            pl.BlockSpec((1, d), lambda b, i: (0, 0)),
        ],
        out_specs=pl.BlockSpec((1, ts, d), lambda b, i: (b, i, 0)),
        out_shape=jax.ShapeDtypeStruct(x.shape, F32),
        compiler_params=pltpu.CompilerParams(
            dimension_semantics=("parallel", "parallel"), vmem_limit_bytes=VMEM_LIMIT),
        name="xattn",
    )(x, kv, w_q.astype(BF16), w_o.astype(BF16), ln_g.reshape(1, -1), ln_b.reshape(1, -1))


def _extract_top(vals, tags, payload, rounds):
    tops, picks = [], []
    for _ in range(rounds):
        m = jnp.max(vals, axis=0, keepdims=True)
        first = jnp.min(jnp.where(vals == m, tags, np.float32(2 ** 24)), axis=0, keepdims=True)
        sel = tags == first
        tops.append(m)
        if payload is None:
            picks.append(first)
        else:
            picks.append(jnp.max(jnp.where(sel, payload, -1.0), axis=0, keepdims=True))
        vals = jnp.where(sel, -jnp.inf, vals)
    return jnp.concatenate(tops, axis=0), jnp.concatenate(picks, axis=0)


def _route_kernel(x_ref, wq_ref, keys_ref, code_ref, idx_ref, gate_ref, idx_t, gate_t):
    tt = x_ref.shape[0]
    xb = x_ref[...].astype(BF16)
    row = lax.broadcasted_iota(jnp.int32, (N_KEYS, tt), 0).astype(F32)
    code = code_ref[...]

    def head(h, carry):
        q = _dot(xb, wq_ref[h]).astype(BF16)
        hk = q.shape[1] // 2
        sa = _dot_nt(keys_ref[h, 0], q[:, :hk])
        sb = _dot_nt(keys_ref[h, 1], q[:, hk:])
        va, ia = _extract_top(sa, row, None, TOPK)
        vb, ib = _extract_top(sb, row, None, TOPK)
        ea = ia * np.float32(N_KEYS)
        cand = [va + vb[0:1], ]
        eid = [ea + ib[0:1], ]
        for j in range(1, SUBLANES):
            cand.append(va[0:SUBLANES] + vb[j:j + 1])
            eid.append(ea[0:SUBLANES] + ib[j:j + 1])
        cand.append(va[0:1] + vb[SUBLANES:])
        eid.append(ea[0:1] + ib[SUBLANES:])
        cand = jnp.where(code >= 0, jnp.concatenate(cand, axis=0), -jnp.inf)
        eid = jnp.concatenate(eid, axis=0)
        top, expert = _extract_top(cand, code, eid, TOPK)
        e = jnp.exp(top - top[0:1])
        r0 = pl.multiple_of(h * TOPK, TOPK)
        gate_t[pl.ds(r0, TOPK), :] = e / jnp.sum(e, axis=0, keepdims=True)
        idx_t[pl.ds(r0, TOPK), :] = expert.astype(jnp.int32)
        return carry

    lax.fori_loop(0, PEER_HEADS, head, 0)
    idx_ref[...] = idx_t[...].T
    gate_ref[...] = gate_t[...].T


def _candidate_codes(tt):
    codes = [[i * TOPK for i in range(TOPK)]]
    for j in range(1, SUBLANES):
        codes.append([i * TOPK + j if (i + 1) * (j + 1) <= TOPK else -1 for i in range(SUBLANES)])
    codes.append(list(range(SUBLANES, TOPK)))
    flat = np.asarray([c for blk in codes for c in blk], dtype=np.float32)
    return jnp.asarray(np.broadcast_to(flat[:, None], (flat.shape[0], tt)))


def _peer_route(xt, w_query, sub_keys, tt):
    n_tok, d = xt.shape
    dkey = w_query.shape[1] // PEER_HEADS
    wq = w_query.reshape(d, PEER_HEADS, dkey).transpose(1, 0, 2).astype(BF16)
    hk = PEER_HEADS * TOPK
    code = _candidate_codes(tt)
    return pl.pallas_call(
        _route_kernel,
        grid=(n_tok // tt,),
        in_specs=[
            pl.BlockSpec((tt, d), lambda i: (i, 0)),
            pl.BlockSpec(wq.shape, lambda i: (0, 0, 0)),
            pl.BlockSpec(sub_keys.shape, lambda i: (0, 0, 0, 0)),
            pl.BlockSpec(code.shape, lambda i: (0, 0)),
        ],
        out_specs=[pl.BlockSpec((tt, hk), lambda i: (i, 0)),
                   pl.BlockSpec((tt, hk), lambda i: (i, 0))],
        out_shape=[jax.ShapeDtypeStruct((n_tok, hk), jnp.int32),
                   jax.ShapeDtypeStruct((n_tok, hk), F32)],
        scratch_shapes=[pltpu.VMEM((hk, tt), jnp.int32), pltpu.VMEM((hk, tt), F32)],
        compiler_params=pltpu.CompilerParams(
            dimension_semantics=("parallel",), vmem_limit_bytes=VMEM_LIMIT),
        name="peer_route",
    )(xt, wq, sub_keys.astype(BF16), code)


ROW_WORDS = 4
TOK_GROUP = 32


def _pack_rows(w):
    n, d = w.shape
    assert d == 2 * ROW_WORDS * LANES
    bits = lax.bitcast_convert_type(w.astype(BF16), jnp.uint16).astype(jnp.uint32)
    bits = bits.reshape(n, ROW_WORDS, 2, LANES)
    return ((bits[:, :, 1, :] << 16) | bits[:, :, 0, :]).reshape(n * ROW_WORDS, LANES)


def _gather_rows(idx_ref, t, tab_ref, hk):
    words = [tab_ref[pl.ds(pl.multiple_of(idx_ref[t, k], ROW_WORDS), ROW_WORDS), :] for k in range(hk)]
    return pltpu.bitcast(jnp.concatenate(words, axis=0), BF16)


def _diag_mask(hk):
    r = lax.broadcasted_iota(jnp.int32, (SUBLANES, hk * SUBLANES), 0)
    c = lax.broadcasted_iota(jnp.int32, (SUBLANES, hk * SUBLANES), 1)
    return (c & (SUBLANES - 1)) == r


def _idx_copy(idx_hbm, row0, buf, sem):
    return pltpu.make_async_copy(idx_hbm.at[pl.ds(row0, TOK_GROUP)], buf, sem)


def _for_each_group(idx_hbm, tb, idx_a, idx_b, sems, process):
    base = pl.program_id(0) * tb
    last = tb - TOK_GROUP
    _idx_copy(idx_hbm, base, idx_a, sems.at[0]).start()

    def pair(i, carry):
        t0 = pl.multiple_of(i * (2 * TOK_GROUP), 2 * TOK_GROUP)
        _idx_copy(idx_hbm, base + t0 + TOK_GROUP, idx_b, sems.at[1]).start()
        _idx_copy(idx_hbm, base, idx_a, sems.at[0]).wait()
        process(idx_a, t0)
        _idx_copy(idx_hbm, base + jnp.minimum(t0 + 2 * TOK_GROUP, last), idx_a, sems.at[0]).start()
        _idx_copy(idx_hbm, base, idx_b, sems.at[1]).wait()
        process(idx_b, t0 + TOK_GROUP)
        return carry

    lax.fori_loop(0, tb // (2 * TOK_GROUP), pair, 0)
    _idx_copy(idx_hbm, base, idx_a, sems.at[0]).wait()


def _idx_scratch(hk):
    return [pltpu.SMEM((TOK_GROUP, hk), jnp.int32), pltpu.SMEM((TOK_GROUP, hk), jnp.int32),
            pltpu.SemaphoreType.DMA((2,))]


def _down_kernel(idx_hbm, x_ref, gate_ref, fold_ref, tab_ref, c_ref, idx_a, idx_b, sems):
    hk = gate_ref.shape[1]
    mask = _diag_mask(hk)

    def process(idx_ref, t0):
        parts = []
        for j in range(TOK_GROUP):
            rows = _gather_rows(idx_ref, j, tab_ref, hk)
            s = _dot_nt(x_ref[t0 + j], rows)
            parts.append(jnp.where(mask, s, 0.0).astype(BF16))
        z = _dot(jnp.concatenate(parts, axis=0), fold_ref[...])
        act = jnp.concatenate(
            [jnp.sum(z[j * SUBLANES:(j + 1) * SUBLANES], axis=0, keepdims=True)
             for j in range(TOK_GROUP)], axis=0)
        c_ref[pl.ds(t0, TOK_GROUP), :] = gate_ref[pl.ds(t0, TOK_GROUP), :] * _gelu(act)

    _for_each_group(idx_hbm, x_ref.shape[0], idx_a, idx_b, sems, process)


def _peer_down(idx, x3, gate, tab, tb):
    n_tok, hk = idx.shape
    assert tb % (2 * TOK_GROUP) == 0
    fold = (jnp.arange(hk * SUBLANES)[:, None] // SUBLANES == jnp.arange(hk)[None, :]).astype(BF16)
    return pl.pallas_call(
        _down_kernel,
        grid=(n_tok // tb,),
        in_specs=[
            pl.BlockSpec(memory_space=pl.ANY),
            pl.BlockSpec((tb, SUBLANES, LANES), lambda i: (i, 0, 0)),
            pl.BlockSpec((tb, hk), lambda i: (i, 0)),
            pl.BlockSpec(fold.shape, lambda i: (0, 0)),
            pl.BlockSpec(memory_space=pltpu.VMEM),
        ],
        out_specs=pl.BlockSpec((tb, hk), lambda i: (i, 0)),
        out_shape=jax.ShapeDtypeStruct((n_tok, hk), F32),
        scratch_shapes=_idx_scratch(hk),
        compiler_params=pltpu.CompilerParams(
            dimension_semantics=("arbitrary",), vmem_limit_bytes=VMEM_LIMIT),
        name="peer_down",
    )(idx, x3, gate, fold, tab)


def _up_kernel(idx_hbm, c_ref, spread_ref, tab_ref, o_ref, idx_a, idx_b, sems):
    hk = c_ref.shape[1]
    mask = _diag_mask(hk)

    def process(idx_ref, t0):
        cexp = _dot(c_ref[pl.ds(t0, TOK_GROUP), :].astype(BF16), spread_ref[...])
        for j in range(TOK_GROUP):
            rows = _gather_rows(idx_ref, j, tab_ref, hk)
            cj = jnp.where(mask, jnp.broadcast_to(cexp[j:j + 1, :], mask.shape), 0.0)
            o_ref[t0 + j] = _dot(cj.astype(BF16), rows)

    _for_each_group(idx_hbm, c_ref.shape[0], idx_a, idx_b, sems, process)


def _peer_up(idx, c, tab, tb):
    n_tok, hk = idx.shape
    assert tb % (2 * TOK_GROUP) == 0
    spread = (jnp.arange(hk)[:, None] == jnp.arange(hk * SUBLANES)[None, :] // SUBLANES).astype(BF16)
    return pl.pallas_call(
        _up_kernel,
        grid=(n_tok // tb,),
        in_specs=[
            pl.BlockSpec(memory_space=pl.ANY),
            pl.BlockSpec((tb, hk), lambda i: (i, 0)),
            pl.BlockSpec(spread.shape, lambda i: (0, 0)),
            pl.BlockSpec(memory_space=pltpu.VMEM),
        ],
        out_specs=pl.BlockSpec((tb, SUBLANES, LANES), lambda i: (i, 0, 0)),
        out_shape=jax.ShapeDtypeStruct((n_tok, SUBLANES, LANES), F32),
        scratch_shapes=_idx_scratch(hk),
        compiler_params=pltpu.CompilerParams(
            dimension_semantics=("arbitrary",), vmem_limit_bytes=VMEM_LIMIT),
        name="peer_up",
    )(idx, c, spread, tab)


def _add_ln_kernel(x_ref, h_ref, g_ref, b_ref, o_ref, *, alpha):
    o_ref[...] = _ln(alpha * x_ref[...] + h_ref[...], g_ref[...], b_ref[...])


def _add_ln(x, h, g, b, alpha, tt):
    n_tok, d = x.shape
    return pl.pallas_call(
        functools.partial(_add_ln_kernel, alpha=alpha),
        grid=(n_tok // tt,),
        in_specs=[pl.BlockSpec((tt, d), lambda i: (i, 0)), pl.BlockSpec((tt, d), lambda i: (i, 0)),
                  pl.BlockSpec((1, d), lambda i: (0, 0)), pl.BlockSpec((1, d), lambda i: (0, 0))],
        out_specs=pl.BlockSpec((tt, d), lambda i: (i, 0)),
        out_shape=jax.ShapeDtypeStruct(x.shape, F32),
        compiler_params=pltpu.CompilerParams(
            dimension_semantics=("parallel",), vmem_limit_bytes=VMEM_LIMIT),
        name="add_ln",
    )(x, h, g.reshape(1, -1), b.reshape(1, -1))


def _tile(n, want):
    return want if n % want == 0 else n


def kernel(x, mem, w_in, b_in, sgu_g, sgu_b, w_spatial, b_spatial, conv_w, w_mix_out, ln1_g, ln1_b,
           w_xq, w_xkv, w_xo, ln2_g, ln2_b, w_pq, sub_keys, w_down, w_up, ln3_g, ln3_b):
    depth = w_in.shape[0]
    bsz, seq, d = x.shape
    assert d == SUBLANES * LANES and seq % CHUNK == 0
    alpha = np.float32((2 * depth) ** 0.25)
    n_tok = bsz * seq
    ts = _tile(seq, 512)
    for l in range(depth):
        x = _mixer(x, w_in[l], b_in[l], sgu_g[l], sgu_b[l], w_spatial[l], b_spatial[l], conv_w[l],
                   w_mix_out[l], ln1_g[l], ln1_b[l], alpha, ts)
        kv = _kv_proj(mem, w_xkv[l])
        x = _xattn(x, kv, w_xq[l], w_xo[l], ln2_g[l], ln2_b[l], alpha, ts)
        xt = x.reshape(n_tok, d)
        idx, gate = _peer_route(xt, w_pq[l], sub_keys[l], _tile(n_tok, 256))
        idx = idx * ROW_WORDS
        tb = _tile(n_tok, 1024)
        x3 = xt.reshape(n_tok, SUBLANES, LANES).astype(BF16)
        c = _peer_down(idx, x3, gate, _pack_rows(w_down[l]), tb)
        h = _peer_up(idx, c, _pack_rows(w_up[l]), tb)
        x = _add_ln(xt, h.reshape(n_tok, d), ln3_g[l], ln3_b[l], alpha, _tile(n_tok, 512))
        x = x.reshape(bsz, seq, d)
    return x
```

```python
import functools

import jax
import jax.numpy as jnp
import numpy as np
from jax import lax
from jax.experimental import pallas as pl
from jax.experimental.pallas import tpu as pltpu

LN_EPS = 1e-5
LANES = 128
SUBLANES = 8
A_HEADS = 4
CHUNK = 128
X_HEADS = 4
PEER_HEADS = 8
N_KEYS = 128
TOPK = 16
ROW_WORDS = 4
VMEM_LIMIT = 56 * 1024 * 1024

F32 = jnp.float32
BF16 = jnp.bfloat16


def _gelu(x):
    return 0.5 * x * (1.0 + lax.erf(x * np.float32(2.0 ** -0.5)))


def _ln(v, g, b):
    mu = jnp.mean(v, axis=-1, keepdims=True)
    d = v - mu
    var = jnp.mean(d * d, axis=-1, keepdims=True)
    return d * lax.rsqrt(var + LN_EPS) * g + b


def _dot(a, b):
    return jnp.dot(a, b, preferred_element_type=F32)


def _dot_nt(a, b):
    return lax.dot_general(a, b, (((1,), (1,)), ((), ())), preferred_element_type=F32)


def _mixer_kernel(x_ref, win_ref, bin_ref, sg_ref, sb_ref, ws_ref, bsp_ref, cw_ref, wout_ref,
                  lg_ref, lb_ref, o_ref, zc_ref, *, alpha, ts, aw):
    @pl.when(pl.program_id(1) == 0)
    def _():
        zc_ref[0:SUBLANES, :] = jnp.zeros((SUBLANES, zc_ref.shape[1]), F32)

    x = x_ref[0]
    xb = x.astype(BF16)

    def zcol(lo, hi):
        return _dot(xb, win_ref[:, lo:hi]) + bin_ref[:, lo:hi]

    hd = aw // A_HEADS
    u = _gelu(zcol(0, aw))
    v = _gelu(zcol(aw, 2 * aw))
    mixed = []
    for h in range(A_HEADS):
        sl = slice(h * hd, (h + 1) * hd)
        vh = _ln(v[:, sl], sg_ref[:, sl], sb_ref[:, sl]).astype(BF16)
        mixed.append(jnp.concatenate(
            [_dot(ws_ref[h], vh[c * CHUNK:(c + 1) * CHUNK]) + bsp_ref[:, sl]
             for c in range(ts // CHUNK)], axis=0))
    ya = u * jnp.concatenate(mixed, axis=-1)

    bw = zc_ref.shape[1]
    hb = zcol(2 * aw, 2 * aw + bw)
    gb = zcol(2 * aw + bw, 2 * aw + 2 * bw)
    gc = zcol(2 * aw + 2 * bw, 2 * aw + 3 * bw)
    zc_ref[SUBLANES:SUBLANES + ts, :] = gc * hb
    conv = (cw_ref[0:1, :] * zc_ref[SUBLANES - 2:SUBLANES - 2 + ts, :]
            + cw_ref[1:2, :] * zc_ref[SUBLANES - 1:SUBLANES - 1 + ts, :]
            + cw_ref[2:3, :] * zc_ref[SUBLANES:SUBLANES + ts, :])
    yb = gb * conv
    zc_ref[0:SUBLANES, :] = zc_ref[ts:ts + SUBLANES, :]

    h = _dot(ya.astype(BF16), wout_ref[0:aw, :]) + _dot(yb.astype(BF16), wout_ref[aw:, :])
    o_ref[0] = _ln(alpha * x + h, lg_ref[...], lb_ref[...])


def _mixer(x, w_in, b_in, sgu_g, sgu_b, w_spatial, b_spatial, conv_w, w_out, ln_g, ln_b, alpha, ts):
    bsz, seq, d = x.shape
    aw = A_HEADS * sgu_g.shape[-1]
    bw = conv_w.shape[-1]
    causal = jnp.tril(jnp.ones((CHUNK, CHUNK), dtype=bool))
    ws = jnp.where(causal[None], w_spatial, 0.0).astype(BF16)
    bsp = jnp.repeat(b_spatial.T, aw // A_HEADS, axis=1)
    const = lambda *shape: pl.BlockSpec(shape, lambda b, i: (0,) * len(shape))
    return pl.pallas_call(
        functools.partial(_mixer_kernel, alpha=alpha, ts=ts, aw=aw),
        grid=(bsz, seq // ts),
        in_specs=[
            pl.BlockSpec((1, ts, d), lambda b, i: (b, i, 0)),
            const(d, w_in.shape[1]), const(1, w_in.shape[1]), const(1, aw), const(1, aw),
            const(A_HEADS, CHUNK, CHUNK), const(CHUNK, aw), const(conv_w.shape[0], bw),
            const(aw + bw, d), const(1, d), const(1, d),
        ],
        out_specs=pl.BlockSpec((1, ts, d), lambda b, i: (b, i, 0)),
        out_shape=jax.ShapeDtypeStruct(x.shape, F32),
        scratch_shapes=[pltpu.VMEM((ts + 2 * SUBLANES, bw), F32)],
        compiler_params=pltpu.CompilerParams(
            dimension_semantics=("parallel", "arbitrary"), vmem_limit_bytes=VMEM_LIMIT),
        name="mixer",
    )(x, w_in.astype(BF16), b_in.reshape(1, -1), sgu_g.reshape(1, -1), sgu_b.reshape(1, -1),
      ws, bsp, conv_w.reshape(conv_w.shape[0], bw), w_out.astype(BF16),
      ln_g.reshape(1, -1), ln_b.reshape(1, -1))


def _kv_kernel(m_ref, w_ref, o_ref):
    o_ref[0] = _dot(m_ref[0].astype(BF16), w_ref[...]).astype(BF16)


def _kv_proj(mem, w_kv):
    bsz, m, d = mem.shape
    n = w_kv.shape[1]
    return pl.pallas_call(
        _kv_kernel,
        grid=(bsz,),
        in_specs=[pl.BlockSpec((1, m, d), lambda b: (b, 0, 0)),
                  pl.BlockSpec((d, n), lambda b: (0, 0))],
        out_specs=pl.BlockSpec((1, m, n), lambda b: (b, 0, 0)),
        out_shape=jax.ShapeDtypeStruct((bsz, m, n), BF16),
        compiler_params=pltpu.CompilerParams(
            dimension_semantics=("parallel",), vmem_limit_bytes=VMEM_LIMIT),
        name="kv_proj",
    )(mem, w_kv.astype(BF16))


def _xattn_kernel(x_ref, kv_ref, wq_ref, wo_ref, lg_ref, lb_ref, o_ref, *, alpha):
    x = x_ref[0]
    d = x.shape[-1]
    dk = d // X_HEADS
    q = _dot(x.astype(BF16), wq_ref[...])
    outs = []
    for h in range(X_HEADS):
        qh = q[:, h * dk:(h + 1) * dk].astype(BF16)
        s = _dot_nt(qh, kv_ref[0, :, h * dk:(h + 1) * dk]) * np.float32(dk ** -0.5)
        e = jnp.exp(s - jnp.max(s, axis=-1, keepdims=True))
        p = e / jnp.sum(e, axis=-1, keepdims=True)
        outs.append(_dot(p.astype(BF16), kv_ref[0, :, d + h * dk:d + (h + 1) * dk]))
    o = jnp.concatenate(outs, axis=-1).astype(BF16)
    o_ref[0] = _ln(alpha * x + _dot(o, wo_ref[...]), lg_ref[...], lb_ref[...])


def _xattn(x, kv, w_q, w_o, ln_g, ln_b, alpha, ts):
    bsz, seq, d = x.shape
    m = kv.shape[1]
    return pl.pallas_call(
        functools.partial(_xattn_kernel, alpha=alpha),
        grid=(bsz, seq // ts),
        in_specs=[
            pl.BlockSpec((1, ts, d), lambda b, i: (b, i, 0)),
            pl.BlockSpec((1, m, 2 * d), lambda b, i: (b, 0, 0)),
            pl.BlockSpec((d, d), lambda b, i: (0, 0)),
            pl.BlockSpec((d, d), lambda b, i: (0, 0)),
            pl.BlockSpec((1, d), lambda b, i: (0, 0)),
            pl.BlockSpec((1, d), lambda b, i: (0, 0)),
        ],
        out_specs=pl.BlockSpec((1, ts, d), lambda b, i: (b, i, 0)),
        out_shape=jax.ShapeDtypeStruct(x.shape, F32),
        compiler_params=pltpu.CompilerParams(
            dimension_semantics=("parallel", "parallel"), vmem_limit_bytes=VMEM_LIMIT),
        name="xattn",
    )(x, kv, w_q.astype(BF16), w_o.astype(BF16), ln_g.reshape(1, -1), ln_b.reshape(1, -1))


def _extract_top(vals, tags, payload, rounds):
    tops, picks = [], []
    for _ in range(rounds):
        m = jnp.max(vals, axis=0, keepdims=True)
        first = jnp.min(jnp.where(vals == m, tags, np.float32(2 ** 24)), axis=0, keepdims=True)
        sel = tags == first
        tops.append(m)
        if payload is None:
            picks.append(first)
        else:
            picks.append(jnp.max(jnp.where(sel, payload, -1.0), axis=0, keepdims=True))
        vals = jnp.where(sel, -jnp.inf, vals)
    return jnp.concatenate(tops, axis=0), jnp.concatenate(picks, axis=0)


def _route_kernel(x_ref, wq_ref, keys_ref, code_ref, idx_ref, gate_ref, idx_t, gate_t, sc_ref, top_ref):
    tt = x_ref.shape[0]
    xb = x_ref[...].astype(BF16)
    row = lax.broadcasted_iota(jnp.int32, (N_KEYS, tt), 0).astype(F32)
    code = code_ref[...]

    q = _dot(xb, wq_ref[...]).astype(BF16)
    hk = q.shape[1] // (2 * PEER_HEADS)
    for i in range(2 * PEER_HEADS):
        sc_ref[i] = _dot_nt(keys_ref[i // 2, i % 2], q[:, i * hk:(i + 1) * hk])
    top_ref[1] = jnp.zeros(top_ref.shape[1:], F32)

    def sub_top(h, slot):
        va, ia = _extract_top(sc_ref[2 * h], row, None, TOPK)
        vb, ib = _extract_top(sc_ref[2 * h + 1], row, None, TOPK)
        top_ref[slot] = jnp.concatenate([va, vb, ia, ib], axis=0)

    def pair_top(h, slot):
        t = top_ref[slot]
        va, vb, ia, ib = (t[i * TOPK:(i + 1) * TOPK] for i in range(4))
        ea = ia * np.float32(N_KEYS)
        cand = [va + vb[0:1], ]
        eid = [ea + ib[0:1], ]
        for j in range(1, SUBLANES):
            cand.append(va[0:SUBLANES] + vb[j:j + 1])
            eid.append(ea[0:SUBLANES] + ib[j:j + 1])
        cand.append(va[0:1] + vb[SUBLANES:])
        eid.append(ea[0:1] + ib[SUBLANES:])
        cand = jnp.where(code >= 0, jnp.concatenate(cand, axis=0), -jnp.inf)
        eid = jnp.concatenate(eid, axis=0)
        top, expert = _extract_top(cand, code, eid, TOPK)
        e = jnp.exp(top - top[0:1])
        r0 = pl.multiple_of(h * TOPK, TOPK)
        gate_t[pl.ds(r0, TOPK), :] = e / jnp.sum(e, axis=0, keepdims=True)
        idx_t[pl.ds(r0, TOPK), :] = expert.astype(jnp.int32) * ROW_WORDS

    def two_heads(p, carry):
        h = 2 * p
        sub_top(h, 0)
        pair_top(jnp.maximum(h - 1, 0), 1)
        sub_top(h + 1, 1)
        pair_top(h, 0)
        return carry

    lax.fori_loop(0, PEER_HEADS // 2, two_heads, 0)
    pair_top(PEER_HEADS - 1, 1)
    idx_ref[...] = idx_t[...].T
    gate_ref[...] = gate_t[...].T


def _candidate_codes(tt):
    codes = [[i * TOPK for i in range(TOPK)]]
    for j in range(1, SUBLANES):
        codes.append([i * TOPK + j if (i + 1) * (j + 1) <= TOPK else -1 for i in range(SUBLANES)])
    codes.append(list(range(SUBLANES, TOPK)))
    flat = np.asarray([c for blk in codes for c in blk], dtype=np.float32)
    return jnp.asarray(np.broadcast_to(flat[:, None], (flat.shape[0], tt)))


def _peer_route(xt, w_query, sub_keys, tt):
    n_tok, d = xt.shape
    wq = w_query.astype(BF16)
    hk = PEER_HEADS * TOPK
    code = _candidate_codes(tt)
    return pl.pallas_call(
        _route_kernel,
        grid=(n_tok // tt,),
        in_specs=[
            pl.BlockSpec((tt, d), lambda i: (i, 0)),
            pl.BlockSpec(wq.shape, lambda i: (0, 0)),
            pl.BlockSpec(sub_keys.shape, lambda i: (0, 0, 0, 0)),
            pl.BlockSpec(code.shape, lambda i: (0, 0)),
        ],
        out_specs=[pl.BlockSpec((tt, hk), lambda i: (i, 0)),
                   pl.BlockSpec((tt, hk), lambda i: (i, 0))],
        out_shape=[jax.ShapeDtypeStruct((n_tok, hk), jnp.int32),
                   jax.ShapeDtypeStruct((n_tok, hk), F32)],
        scratch_shapes=[pltpu.VMEM((hk, tt), jnp.int32), pltpu.VMEM((hk, tt), F32),
                        pltpu.VMEM((2 * PEER_HEADS, N_KEYS, tt), F32), pltpu.VMEM((2, 4 * TOPK, tt), F32)],
        compiler_params=pltpu.CompilerParams(
            dimension_semantics=("parallel",), vmem_limit_bytes=VMEM_LIMIT),
        name="peer_route",
    )(xt, wq, sub_keys.astype(BF16), code)


TOK_GROUP = 32


def _pack_kernel(w_ref, o_ref):
    n = w_ref.shape[0]
    for s in range(ROW_WORDS):
        lo = w_ref[:, 2 * s * LANES:(2 * s + 1) * LANES].astype(BF16).astype(F32)
        hi = w_ref[:, (2 * s + 1) * LANES:(2 * s + 2) * LANES].astype(BF16).astype(F32)
        word = pltpu.bitcast(hi, jnp.uint32) | (pltpu.bitcast(lo, jnp.uint32) >> 16)
        o_ref[pl.ds(s, n, stride=ROW_WORDS), :] = word


def _pack_rows(w, rows):
    n, d = w.shape
    assert d == 2 * ROW_WORDS * LANES and n % rows == 0
    return pl.pallas_call(
        _pack_kernel,
        grid=(n // rows,),
        in_specs=[pl.BlockSpec((rows, d), lambda i: (i, 0))],
        out_specs=pl.BlockSpec((rows * ROW_WORDS, LANES), lambda i: (i, 0)),
        out_shape=jax.ShapeDtypeStruct((n * ROW_WORDS, LANES), jnp.uint32),
        compiler_params=pltpu.CompilerParams(
            dimension_semantics=("parallel",), vmem_limit_bytes=VMEM_LIMIT),
        name="pack_rows",
    )(w)


def _gather_rows(idx_ref, t, tab_ref, hk):
    words = [tab_ref[pl.ds(pl.multiple_of(idx_ref[t, k], ROW_WORDS), ROW_WORDS), :] for k in range(hk)]
    return pltpu.bitcast(jnp.concatenate(words, axis=0), BF16)


def _diag_mask(hk):
    r = lax.broadcasted_iota(jnp.int32, (SUBLANES, hk * SUBLANES), 0)
    c = lax.broadcasted_iota(jnp.int32, (SUBLANES, hk * SUBLANES), 1)
    return (c & (SUBLANES - 1)) == r


def _idx_copy(idx_hbm, row0, buf, sem):
    return pltpu.make_async_copy(idx_hbm.at[pl.ds(row0, TOK_GROUP)], buf, sem)


def _for_each_group(idx_hbm, tb, idx_a, idx_b, sems, process):
    base = pl.program_id(0) * tb
    last = tb - TOK_GROUP
    _idx_copy(idx_hbm, base, idx_a, sems.at[0]).start()

    def pair(i, carry):
        t0 = pl.multiple_of(i * (2 * TOK_GROUP), 2 * TOK_GROUP)
        _idx_copy(idx_hbm, base + t0 + TOK_GROUP, idx_b, sems.at[1]).start()
        _idx_copy(idx_hbm, base, idx_a, sems.at[0]).wait()
        process(idx_a, t0)
        _idx_copy(idx_hbm, base + jnp.minimum(t0 + 2 * TOK_GROUP, last), idx_a, sems.at[0]).start()
        _idx_copy(idx_hbm, base, idx_b, sems.at[1]).wait()
        process(idx_b, t0 + TOK_GROUP)
        return carry

    lax.fori_loop(0, tb // (2 * TOK_GROUP), pair, 0)
    _idx_copy(idx_hbm, base, idx_a, sems.at[0]).wait()


def _idx_scratch(hk):
    return [pltpu.SMEM((TOK_GROUP, hk), jnp.int32), pltpu.SMEM((TOK_GROUP, hk), jnp.int32),
            pltpu.SemaphoreType.DMA((2,))]


def _down_kernel(idx_hbm, x_ref, gate_ref, fold_ref, tab_ref, c_ref, idx_a, idx_b, sems, xt_ref):
    hk = gate_ref.shape[1]
    mask = _diag_mask(hk)

    def process(idx_ref, t0):
        xs = x_ref[pl.ds(t0, TOK_GROUP), :]
        for r in range(SUBLANES):
            xt_ref[pl.ds(r, TOK_GROUP, stride=SUBLANES), :] = xs[:, r * LANES:(r + 1) * LANES]
        parts = []
        for j in range(TOK_GROUP):
            rows = _gather_rows(idx_ref, j, tab_ref, hk)
            xj = xt_ref[j * SUBLANES:(j + 1) * SUBLANES, :].astype(BF16)
            s = _dot_nt(xj, rows)
            parts.append(jnp.where(mask, s, 0.0).astype(BF16))
        z = _dot(jnp.concatenate(parts, axis=0), fold_ref[...])
        act = jnp.concatenate(
            [jnp.sum(z[j * SUBLANES:(j + 1) * SUBLANES], axis=0, keepdims=True)
             for j in range(TOK_GROUP)], axis=0)
        c_ref[pl.ds(t0, TOK_GROUP), :] = gate_ref[pl.ds(t0, TOK_GROUP), :] * _gelu(act)

    _for_each_group(idx_hbm, x_ref.shape[0], idx_a, idx_b, sems, process)


def _peer_down(idx, xt, gate, tab, tb):
    n_tok, hk = idx.shape
    assert tb % (2 * TOK_GROUP) == 0 and xt.shape[1] == SUBLANES * LANES
    fold = (jnp.arange(hk * SUBLANES)[:, None] // SUBLANES == jnp.arange(hk)[None, :]).astype(BF16)
    return pl.pallas_call(
        _down_kernel,
        grid=(n_tok // tb,),
        in_specs=[
            pl.BlockSpec(memory_space=pl.ANY),
            pl.BlockSpec((tb, SUBLANES * LANES), lambda i: (i, 0)),
            pl.BlockSpec((tb, hk), lambda i: (i, 0)),
            pl.BlockSpec(fold.shape, lambda i: (0, 0)),
            pl.BlockSpec(memory_space=pltpu.VMEM),
        ],
        out_specs=pl.BlockSpec((tb, hk), lambda i: (i, 0)),
        out_shape=jax.ShapeDtypeStruct((n_tok, hk), F32),
        scratch_shapes=_idx_scratch(hk) + [pltpu.VMEM((TOK_GROUP * SUBLANES, LANES), F32)],
        compiler_params=pltpu.CompilerParams(
            dimension_semantics=("arbitrary",), vmem_limit_bytes=VMEM_LIMIT),
        name="peer_down",
    )(idx, xt, gate, fold, tab)


def _up_kernel(idx_hbm, c_ref, x_ref, spread_ref, tab_ref, g_ref, b_ref, o_ref, idx_a, idx_b, sems, ht_ref,
               *, alpha):
    hk = c_ref.shape[1]
    mask = _diag_mask(hk)

    def process(idx_ref, t0):
        cexp = _dot(c_ref[pl.ds(t0, TOK_GROUP), :].astype(BF16), spread_ref[...])
        for j in range(TOK_GROUP):
            rows = _gather_rows(idx_ref, j, tab_ref, hk)
            cj = jnp.where(mask, jnp.broadcast_to(cexp[j:j + 1, :], mask.shape), 0.0)
            ht_ref[j * SUBLANES:(j + 1) * SUBLANES, :] = _dot(cj.astype(BF16), rows)
        h = jnp.concatenate([ht_ref[pl.ds(r, TOK_GROUP, stride=SUBLANES), :] for r in range(SUBLANES)],
                            axis=-1)
        o_ref[pl.ds(t0, TOK_GROUP), :] = _ln(alpha * x_ref[pl.ds(t0, TOK_GROUP), :] + h,
                                              g_ref[...], b_ref[...])

    _for_each_group(idx_hbm, c_ref.shape[0], idx_a, idx_b, sems, process)


def _peer_up(idx, c, xt, tab, ln_g, ln_b, alpha, tb):
    n_tok, hk = idx.shape
    d = xt.shape[1]
    assert tb % (2 * TOK_GROUP) == 0 and d == SUBLANES * LANES
    spread = (jnp.arange(hk)[:, None] == jnp.arange(hk * SUBLANES)[None, :] // SUBLANES).astype(BF16)
    return pl.pallas_call(
        functools.partial(_up_kernel, alpha=alpha),
        grid=(n_tok // tb,),
        in_specs=[
            pl.BlockSpec(memory_space=pl.ANY),
            pl.BlockSpec((tb, hk), lambda i: (i, 0)),
            pl.BlockSpec((tb, d), lambda i: (i, 0)),
            pl.BlockSpec(spread.shape, lambda i: (0, 0)),
            pl.BlockSpec(memory_space=pltpu.VMEM),
            pl.BlockSpec((1, d), lambda i: (0, 0)),
            pl.BlockSpec((1, d), lambda i: (0, 0)),
        ],
        out_specs=pl.BlockSpec((tb, d), lambda i: (i, 0)),
        out_shape=jax.ShapeDtypeStruct((n_tok, d), F32),
        scratch_shapes=_idx_scratch(hk) + [pltpu.VMEM((TOK_GROUP * SUBLANES, LANES), F32)],
        compiler_params=pltpu.CompilerParams(
            dimension_semantics=("arbitrary",), vmem_limit_bytes=VMEM_LIMIT),
        name="peer_up",
    )(idx, c, xt, spread, tab, ln_g.reshape(1, -1), ln_b.reshape(1, -1))


def _tile(n, want):
    return want if n % want == 0 else n


def kernel(x, mem, w_in, b_in, sgu_g, sgu_b, w_spatial, b_spatial, conv_w, w_mix_out, ln1_g, ln1_b,
           w_xq, w_xkv, w_xo, ln2_g, ln2_b, w_pq, sub_keys, w_down, w_up, ln3_g, ln3_b):
    depth = w_in.shape[0]
    bsz, seq, d = x.shape
    assert d == SUBLANES * LANES and seq % CHUNK == 0
    alpha = np.float32((2 * depth) ** 0.25)
    n_tok = bsz * seq
    ts = _tile(seq, 512)
    for l in range(depth):
        x = _mixer(x, w_in[l], b_in[l], sgu_g[l], sgu_b[l], w_spatial[l], b_spatial[l], conv_w[l],
                   w_mix_out[l], ln1_g[l], ln1_b[l], alpha, ts)
        kv = _kv_proj(mem, w_xkv[l])
        x = _xattn(x, kv, w_xq[l], w_xo[l], ln2_g[l], ln2_b[l], alpha, ts)
        xt = x.reshape(n_tok, d)
        idx, gate = _peer_route(xt, w_pq[l], sub_keys[l], _tile(n_tok, 256))
        tb = _tile(n_tok, 512)
        pack = _tile(w_down.shape[1], 512)
        c = _peer_down(idx, xt, gate, _pack_rows(w_down[l], pack), tb)
        x = _peer_up(idx, c, xt, _pack_rows(w_up[l], pack), ln3_g[l], ln3_b[l], alpha, tb)
        x = x.reshape(bsz, seq, d)
    return x
```

```python
import functools

import jax
import jax.numpy as jnp
import numpy as np
from jax import lax
from jax.experimental import pallas as pl
from jax.experimental.pallas import tpu as pltpu

LN_EPS = 1e-5
LANES = 128
SUBLANES = 8
A_HEADS = 4
CHUNK = 128
X_HEADS = 4
PEER_HEADS = 8
N_KEYS = 128
TOPK = 16
ROW_WORDS = 4
VMEM_LIMIT = 56 * 1024 * 1024

F32 = jnp.float32
BF16 = jnp.bfloat16


def _gelu(x):
    return 0.5 * x * (1.0 + lax.erf(x * np.float32(2.0 ** -0.5)))


def _ln(v, g, b):
    mu = jnp.mean(v, axis=-1, keepdims=True)
    d = v - mu
    var = jnp.mean(d * d, axis=-1, keepdims=True)
    return d * lax.rsqrt(var + LN_EPS) * g + b


def _dot(a, b):
    return jnp.dot(a, b, preferred_element_type=F32)


def _dot_nt(a, b):
    return lax.dot_general(a, b, (((1,), (1,)), ((), ())), preferred_element_type=F32)


def _mixer_kernel(x_ref, win_ref, bin_ref, sg_ref, sb_ref, ws_ref, bsp_ref, cw_ref, wout_ref,
                  lg_ref, lb_ref, o_ref, zc_ref, *, alpha, ts, aw):
    @pl.when(pl.program_id(1) == 0)
    def _():
        zc_ref[0:SUBLANES, :] = jnp.zeros((SUBLANES, zc_ref.shape[1]), F32)

    x = x_ref[0]
    xb = x.astype(BF16)

    def zcol(lo, hi):
        return _dot(xb, win_ref[:, lo:hi]) + bin_ref[:, lo:hi]

    hd = aw // A_HEADS
    u = _gelu(zcol(0, aw))
    v = _gelu(zcol(aw, 2 * aw))
    mixed = []
    for h in range(A_HEADS):
        sl = slice(h * hd, (h + 1) * hd)
        vh = _ln(v[:, sl], sg_ref[:, sl], sb_ref[:, sl]).astype(BF16)
        mixed.append(jnp.concatenate(
            [_dot(ws_ref[h], vh[c * CHUNK:(c + 1) * CHUNK]) + bsp_ref[:, sl]
             for c in range(ts // CHUNK)], axis=0))
    ya = u * jnp.concatenate(mixed, axis=-1)

    bw = zc_ref.shape[1]
    hb = zcol(2 * aw, 2 * aw + bw)
    gb = zcol(2 * aw + bw, 2 * aw + 2 * bw)
    gc = zcol(2 * aw + 2 * bw, 2 * aw + 3 * bw)
    zc_ref[SUBLANES:SUBLANES + ts, :] = gc * hb
    conv = (cw_ref[0:1, :] * zc_ref[SUBLANES - 2:SUBLANES - 2 + ts, :]
            + cw_ref[1:2, :] * zc_ref[SUBLANES - 1:SUBLANES - 1 + ts, :]
            + cw_ref[2:3, :] * zc_ref[SUBLANES:SUBLANES + ts, :])
    yb = gb * conv
    zc_ref[0:SUBLANES, :] = zc_ref[ts:ts + SUBLANES, :]

    h = _dot(ya.astype(BF16), wout_ref[0:aw, :]) + _dot(yb.astype(BF16), wout_ref[aw:, :])
    o_ref[0] = _ln(alpha * x + h, lg_ref[...], lb_ref[...])


def _mixer(x, w_in, b_in, sgu_g, sgu_b, w_spatial, b_spatial, conv_w, w_out, ln_g, ln_b, alpha, ts):
    bsz, seq, d = x.shape
    aw = A_HEADS * sgu_g.shape[-1]
    bw = conv_w.shape[-1]
    causal = jnp.tril(jnp.ones((CHUNK, CHUNK), dtype=bool))
    ws = jnp.where(causal[None], w_spatial, 0.0).astype(BF16)
    bsp = jnp.repeat(b_spatial.T, aw // A_HEADS, axis=1)
    const = lambda *shape: pl.BlockSpec(shape, lambda b, i: (0,) * len(shape))
    return pl.pallas_call(
        functools.partial(_mixer_kernel, alpha=alpha, ts=ts, aw=aw),
        grid=(bsz, seq // ts),
        in_specs=[
            pl.BlockSpec((1, ts, d), lambda b, i: (b, i, 0)),
            const(d, w_in.shape[1]), const(1, w_in.shape[1]), const(1, aw), const(1, aw),
            const(A_HEADS, CHUNK, CHUNK), const(CHUNK, aw), const(conv_w.shape[0], bw),
            const(aw + bw, d), const(1, d), const(1, d),
        ],
        out_specs=pl.BlockSpec((1, ts, d), lambda b, i: (b, i, 0)),
        out_shape=jax.ShapeDtypeStruct(x.shape, F32),
        scratch_shapes=[pltpu.VMEM((ts + 2 * SUBLANES, bw), F32)],
        compiler_params=pltpu.CompilerParams(
            dimension_semantics=("parallel", "arbitrary"), vmem_limit_bytes=VMEM_LIMIT),
        name="mixer",
    )(x, w_in.astype(BF16), b_in.reshape(1, -1), sgu_g.reshape(1, -1), sgu_b.reshape(1, -1),
      ws, bsp, conv_w.reshape(conv_w.shape[0], bw), w_out.astype(BF16),
      ln_g.reshape(1, -1), ln_b.reshape(1, -1))


def _kv_kernel(m_ref, w_ref, o_ref):
    o_ref[0] = _dot(m_ref[0].astype(BF16), w_ref[...]).astype(BF16)


def _kv_proj(mem, w_kv):
    bsz, m, d = mem.shape
    n = w_kv.shape[1]
    return pl.pallas_call(
        _kv_kernel,
        grid=(bsz,),
        in_specs=[pl.BlockSpec((1, m, d), lambda b: (b, 0, 0)),
                  pl.BlockSpec((d, n), lambda b: (0, 0))],
        out_specs=pl.BlockSpec((1, m, n), lambda b: (b, 0, 0)),
        out_shape=jax.ShapeDtypeStruct((bsz, m, n), BF16),
        compiler_params=pltpu.CompilerParams(
            dimension_semantics=("parallel",), vmem_limit_bytes=VMEM_LIMIT),
        name="kv_proj",
    )(mem, w_kv.astype(BF16))


def _xattn_kernel(x_ref, kv_ref, wq_ref, wo_ref, lg_ref, lb_ref, o_ref, *, alpha):
    x = x_ref[0]
    d = x.shape[-1]
    dk = d // X_HEADS
    q = _dot(x.astype(BF16), wq_ref[...])
    outs = []
    for h in range(X_HEADS):
        qh = q[:, h * dk:(h + 1) * dk].astype(BF16)
        s = _dot_nt(qh, kv_ref[0, :, h * dk:(h + 1) * dk]) * np.float32(dk ** -0.5)
        e = jnp.exp(s - jnp.max(s, axis=-1, keepdims=True))
        p = e / jnp.sum(e, axis=-1, keepdims=True)
        outs.append(_dot(p.astype(BF16), kv_ref[0, :, d + h * dk:d + (h + 1) * dk]))
    o = jnp.concatenate(outs, axis=-1).astype(BF16)
    o_ref[0] = _ln(alpha * x + _dot(o, wo_ref[...]), lg_ref[...], lb_ref[...])


def _xattn(x, kv, w_q, w_o, ln_g, ln_b, alpha, ts):
    bsz, seq, d = x.shape
    m = kv.shape[1]
    return pl.pallas_call(
        functools.partial(_xattn_kernel, alpha=alpha),
        grid=(bsz, seq // ts),
        in_specs=[
            pl.BlockSpec((1, ts, d), lambda b, i: (b, i, 0)),
            pl.BlockSpec((1, m, 2 * d), lambda b, i: (b, 0, 0)),
            pl.BlockSpec((d, d), lambda b, i: (0, 0)),
            pl.BlockSpec((d, d), lambda b, i: (0, 0)),
            pl.BlockSpec((1, d), lambda b, i: (0, 0)),
            pl.BlockSpec((1, d), lambda b, i: (0, 0)),
        ],
        out_specs=pl.BlockSpec((1, ts, d), lambda b, i: (b, i, 0)),
        out_shape=jax.ShapeDtypeStruct(x.shape, F32),
        compiler_params=pltpu.CompilerParams(
            dimension_semantics=("parallel", "parallel"), vmem_limit_bytes=VMEM_LIMIT),
        name="xattn",
    )(x, kv, w_q.astype(BF16), w_o.astype(BF16), ln_g.reshape(1, -1), ln_b.reshape(1, -1))


def _extract_top(vals, tags, payload, rounds):
    tops, picks = [], []
    for _ in range(rounds):
        m = jnp.max(vals, axis=0, keepdims=True)
        first = jnp.min(jnp.where(vals == m, tags, np.float32(2 ** 24)), axis=0, keepdims=True)
        sel = tags == first
        tops.append(m)
        if payload is None:
            picks.append(first)
        else:
            picks.append(jnp.max(jnp.where(sel, payload, -1.0), axis=0, keepdims=True))
        vals = jnp.where(sel, -jnp.inf, vals)
    return jnp.concatenate(tops, axis=0), jnp.concatenate(picks, axis=0)


def _route_kernel(x_ref, wq_ref, keys_ref, code_ref, idx_ref, gate_ref, idx_t, gate_t, sc_ref, top_ref):
    tt = x_ref.shape[0]
    xb = x_ref[...].astype(BF16)
    row = lax.broadcasted_iota(jnp.int32, (N_KEYS, tt), 0).astype(F32)
    code = code_ref[...]

    q = _dot(xb, wq_ref[...]).astype(BF16)
    hk = q.shape[1] // (2 * PEER_HEADS)
    for i in range(2 * PEER_HEADS):
        sc_ref[i] = _dot_nt(keys_ref[i // 2, i % 2], q[:, i * hk:(i + 1) * hk])
    top_ref[1] = jnp.zeros(top_ref.shape[1:], F32)

    def sub_top(h, slot):
        va, ia = _extract_top(sc_ref[2 * h], row, None, TOPK)
        vb, ib = _extract_top(sc_ref[2 * h + 1], row, None, TOPK)
        top_ref[slot] = jnp.concatenate([va, vb, ia, ib], axis=0)

    def pair_top(h, slot):
        t = top_ref[slot]
        va, vb, ia, ib = (t[i * TOPK:(i + 1) * TOPK] for i in range(4))
        ea = ia * np.float32(N_KEYS)
        cand = [va + vb[0:1], ]
        eid = [ea + ib[0:1], ]
        for j in range(1, SUBLANES):
            cand.append(va[0:SUBLANES] + vb[j:j + 1])
            eid.append(ea[0:SUBLANES] + ib[j:j + 1])
        cand.append(va[0:1] + vb[SUBLANES:])
        eid.append(ea[0:1] + ib[SUBLANES:])
        cand = jnp.where(code >= 0, jnp.concatenate(cand, axis=0), -jnp.inf)
        eid = jnp.concatenate(eid, axis=0)
        top, expert = _extract_top(cand, code, eid, TOPK)
        e = jnp.exp(top - top[0:1])
        r0 = pl.multiple_of(h * TOPK, TOPK)
        gate_t[pl.ds(r0, TOPK), :] = e / jnp.sum(e, axis=0, keepdims=True)
        idx_t[pl.ds(r0, TOPK), :] = expert.astype(jnp.int32) * ROW_WORDS

    def two_heads(p, carry):
        h = 2 * p
        sub_top(h, 0)
        pair_top(jnp.maximum(h - 1, 0), 1)
        sub_top(h + 1, 1)
        pair_top(h, 0)
        return carry

    lax.fori_loop(0, PEER_HEADS // 2, two_heads, 0)
    pair_top(PEER_HEADS - 1, 1)
    idx_ref[...] = idx_t[...].T
    gate_ref[...] = gate_t[...].T


def _candidate_codes(tt):
    codes = [[i * TOPK for i in range(TOPK)]]
    for j in range(1, SUBLANES):
        codes.append([i * TOPK + j if (i + 1) * (j + 1) <= TOPK else -1 for i in range(SUBLANES)])
    codes.append(list(range(SUBLANES, TOPK)))
    flat = np.asarray([c for blk in codes for c in blk], dtype=np.float32)
    return jnp.asarray(np.broadcast_to(flat[:, None], (flat.shape[0], tt)))


def _peer_route(xt, w_query, sub_keys, tt):
    n_tok, d = xt.shape
    wq = w_query.astype(BF16)
    hk = PEER_HEADS * TOPK
    code = _candidate_codes(tt)
    return pl.pallas_call(
        _route_kernel,
        grid=(n_tok // tt,),
        in_specs=[
            pl.BlockSpec((tt, d), lambda i: (i, 0)),
            pl.BlockSpec(wq.shape, lambda i: (0, 0)),
            pl.BlockSpec(sub_keys.shape, lambda i: (0, 0, 0, 0)),
            pl.BlockSpec(code.shape, lambda i: (0, 0)),
        ],
        out_specs=[pl.BlockSpec((tt, hk), lambda i: (i, 0)),
                   pl.BlockSpec((tt, hk), lambda i: (i, 0))],
        out_shape=[jax.ShapeDtypeStruct((n_tok, hk), jnp.int32),
                   jax.ShapeDtypeStruct((n_tok, hk), F32)],
        scratch_shapes=[pltpu.VMEM((hk, tt), jnp.int32), pltpu.VMEM((hk, tt), F32),
                        pltpu.VMEM((2 * PEER_HEADS, N_KEYS, tt), F32), pltpu.VMEM((2, 4 * TOPK, tt), F32)],
        compiler_params=pltpu.CompilerParams(
            dimension_semantics=("parallel",), vmem_limit_bytes=VMEM_LIMIT),
        name="peer_route",
    )(xt, wq, sub_keys.astype(BF16), code)


TOK_GROUP = 32


def _pack_kernel(w_ref, o_ref):
    n = w_ref.shape[0]
    for s in range(ROW_WORDS):
        lo = w_ref[:, 2 * s * LANES:(2 * s + 1) * LANES].astype(BF16).astype(F32)
        hi = w_ref[:, (2 * s + 1) * LANES:(2 * s + 2) * LANES].astype(BF16).astype(F32)
        word = pltpu.bitcast(hi, jnp.uint32) | (pltpu.bitcast(lo, jnp.uint32) >> 16)
        o_ref[pl.ds(s, n, stride=ROW_WORDS), :] = word


def _pack_rows(w, rows):
    n, d = w.shape
    assert d == 2 * ROW_WORDS * LANES and n % rows == 0
    return pl.pallas_call(
        _pack_kernel,
        grid=(n // rows,),
        in_specs=[pl.BlockSpec((rows, d), lambda i: (i, 0))],
        out_specs=pl.BlockSpec((rows * ROW_WORDS, LANES), lambda i: (i, 0)),
        out_shape=jax.ShapeDtypeStruct((n * ROW_WORDS, LANES), jnp.uint32),
        compiler_params=pltpu.CompilerParams(
            dimension_semantics=("parallel",), vmem_limit_bytes=VMEM_LIMIT),
        name="pack_rows",
    )(w)


def _gather_rows(idx_ref, t, tab_ref, hk):
    words = [tab_ref[pl.ds(pl.multiple_of(idx_ref[t, k], ROW_WORDS), ROW_WORDS), :] for k in range(hk)]
    return pltpu.bitcast(jnp.concatenate(words, axis=0), BF16)


def _diag_mask(hk):
    r = lax.broadcasted_iota(jnp.int32, (SUBLANES, hk * SUBLANES), 0)
    c = lax.broadcasted_iota(jnp.int32, (SUBLANES, hk * SUBLANES), 1)
    return (c & (SUBLANES - 1)) == r


def _idx_copy(idx_hbm, row0, buf, sem):
    return pltpu.make_async_copy(idx_hbm.at[pl.ds(row0, TOK_GROUP)], buf, sem)


def _for_each_group(idx_hbm, tb, idx_a, idx_b, sems, process):
    step = pl.program_id(0)
    base = step * tb
    last = idx_hbm.shape[0] - TOK_GROUP

    @pl.when(step == 0)
    def _():
        _idx_copy(idx_hbm, 0, idx_a, sems.at[0]).start()

    def pair(i, carry):
        t0 = pl.multiple_of(i * (2 * TOK_GROUP), 2 * TOK_GROUP)
        _idx_copy(idx_hbm, base + t0 + TOK_GROUP, idx_b, sems.at[1]).start()
        _idx_copy(idx_hbm, 0, idx_a, sems.at[0]).wait()
        process(idx_a, t0, 0)
        _idx_copy(idx_hbm, jnp.minimum(base + t0 + 2 * TOK_GROUP, last), idx_a, sems.at[0]).start()
        _idx_copy(idx_hbm, 0, idx_b, sems.at[1]).wait()
        process(idx_b, t0 + TOK_GROUP, 1)
        return carry

    lax.fori_loop(0, tb // (2 * TOK_GROUP), pair, 0)

    @pl.when(step == pl.num_programs(0) - 1)
    def _():
        _idx_copy(idx_hbm, 0, idx_a, sems.at[0]).wait()


def _idx_scratch(hk):
    return [pltpu.SMEM((TOK_GROUP, hk), jnp.int32), pltpu.SMEM((TOK_GROUP, hk), jnp.int32),
            pltpu.SemaphoreType.DMA((2,))]


def _down_kernel(idx_hbm, x_ref, gate_ref, fold_ref, tab_ref, c_ref, idx_a, idx_b, sems, xt_ref,
                 part_a, part_b):
    hk = gate_ref.shape[1]
    tb = x_ref.shape[0]
    mask = _diag_mask(hk)
    parts = (part_a, part_b)

    def finish(part_ref, t0):
        z = _dot(part_ref[...].astype(BF16), fold_ref[...])
        act = jnp.concatenate(
            [jnp.sum(z[j * SUBLANES:(j + 1) * SUBLANES], axis=0, keepdims=True)
             for j in range(TOK_GROUP)], axis=0)
        c_ref[pl.ds(t0, TOK_GROUP), :] = gate_ref[pl.ds(t0, TOK_GROUP), :] * _gelu(act)

    def process(idx_ref, t0, slot):
        finish(parts[1 - slot], pl.multiple_of(jnp.maximum(t0 - TOK_GROUP, 0), TOK_GROUP))
        xs = x_ref[pl.ds(t0, TOK_GROUP), :]
        for r in range(SUBLANES):
            xt_ref[pl.ds(r, TOK_GROUP, stride=SUBLANES), :] = xs[:, r * LANES:(r + 1) * LANES]
        for j in range(TOK_GROUP):
            rows = _gather_rows(idx_ref, j, tab_ref, hk)
            xj = xt_ref[j * SUBLANES:(j + 1) * SUBLANES, :].astype(BF16)
            s = _dot_nt(xj, rows)
            parts[slot][j * SUBLANES:(j + 1) * SUBLANES, :] = jnp.where(mask, s, 0.0)

    part_b[...] = jnp.zeros(part_b.shape, F32)
    _for_each_group(idx_hbm, tb, idx_a, idx_b, sems, process)
    finish(part_b, tb - TOK_GROUP)


def _peer_down(idx, xt, gate, tab, tb):
    n_tok, hk = idx.shape
    assert tb % (2 * TOK_GROUP) == 0 and xt.shape[1] == SUBLANES * LANES
    fold = (jnp.arange(hk * SUBLANES)[:, None] // SUBLANES == jnp.arange(hk)[None, :]).astype(BF16)
    return pl.pallas_call(
        _down_kernel,
        grid=(n_tok // tb,),
        in_specs=[
            pl.BlockSpec(memory_space=pl.ANY),
            pl.BlockSpec((tb, SUBLANES * LANES), lambda i: (i, 0)),
            pl.BlockSpec((tb, hk), lambda i: (i, 0)),
            pl.BlockSpec(fold.shape, lambda i: (0, 0)),
            pl.BlockSpec(memory_space=pltpu.VMEM),
        ],
        out_specs=pl.BlockSpec((tb, hk), lambda i: (i, 0)),
        out_shape=jax.ShapeDtypeStruct((n_tok, hk), F32),
        scratch_shapes=_idx_scratch(hk) + [pltpu.VMEM((TOK_GROUP * SUBLANES, LANES), F32)]
        + [pltpu.VMEM((TOK_GROUP * SUBLANES, hk * SUBLANES), F32) for _ in range(2)],
        compiler_params=pltpu.CompilerParams(
            dimension_semantics=("arbitrary",), vmem_limit_bytes=VMEM_LIMIT),
        name="peer_down",
    )(idx, xt, gate, fold, tab)


def _up_kernel(idx_hbm, c_ref, x_ref, spread_ref, tab_ref, g_ref, b_ref, o_ref, idx_a, idx_b, sems,
               ht_a, ht_b, *, alpha):
    hk = c_ref.shape[1]
    tb = c_ref.shape[0]
    mask = _diag_mask(hk)
    hts = (ht_a, ht_b)

    def finish(ht_ref, t0):
        h = jnp.concatenate([ht_ref[pl.ds(r, TOK_GROUP, stride=SUBLANES), :] for r in range(SUBLANES)],
                            axis=-1)
        o_ref[pl.ds(t0, TOK_GROUP), :] = _ln(alpha * x_ref[pl.ds(t0, TOK_GROUP), :] + h,
                                              g_ref[...], b_ref[...])

    def process(idx_ref, t0, slot):
        finish(hts[1 - slot], pl.multiple_of(jnp.maximum(t0 - TOK_GROUP, 0), TOK_GROUP))
        cexp = _dot(c_ref[pl.ds(t0, TOK_GROUP), :].astype(BF16), spread_ref[...])
        for j in range(TOK_GROUP):
            rows = _gather_rows(idx_ref, j, tab_ref, hk)
            cj = jnp.where(mask, jnp.broadcast_to(cexp[j:j + 1, :], mask.shape), 0.0)
            hts[slot][j * SUBLANES:(j + 1) * SUBLANES, :] = _dot(cj.astype(BF16), rows)

    ht_b[...] = jnp.zeros(ht_b.shape, F32)
    _for_each_group(idx_hbm, tb, idx_a, idx_b, sems, process)
    finish(ht_b, tb - TOK_GROUP)


def _peer_up(idx, c, xt, tab, ln_g, ln_b, alpha, tb):
    n_tok, hk = idx.shape
    d = xt.shape[1]
    assert tb % (2 * TOK_GROUP) == 0 and d == SUBLANES * LANES
    spread = (jnp.arange(hk)[:, None] == jnp.arange(hk * SUBLANES)[None, :] // SUBLANES).astype(BF16)
    return pl.pallas_call(
        functools.partial(_up_kernel, alpha=alpha),
        grid=(n_tok // tb,),
        in_specs=[
            pl.BlockSpec(memory_space=pl.ANY),
            pl.BlockSpec((tb, hk), lambda i: (i, 0)),
            pl.BlockSpec((tb, d), lambda i: (i, 0)),
            pl.BlockSpec(spread.shape, lambda i: (0, 0)),
            pl.BlockSpec(memory_space=pltpu.VMEM),
            pl.BlockSpec((1, d), lambda i: (0, 0)),
            pl.BlockSpec((1, d), lambda i: (0, 0)),
        ],
        out_specs=pl.BlockSpec((tb, d), lambda i: (i, 0)),
        out_shape=jax.ShapeDtypeStruct((n_tok, d), F32),
        scratch_shapes=_idx_scratch(hk)
        + [pltpu.VMEM((TOK_GROUP * SUBLANES, LANES), F32) for _ in range(2)],
        compiler_params=pltpu.CompilerParams(
            dimension_semantics=("arbitrary",), vmem_limit_bytes=VMEM_LIMIT),
        name="peer_up",
    )(idx, c, xt, spread, tab, ln_g.reshape(1, -1), ln_b.reshape(1, -1))


def _tile(n, want):
    return want if n % want == 0 else n


def kernel(x, mem, w_in, b_in, sgu_g, sgu_b, w_spatial, b_spatial, conv_w, w_mix_out, ln1_g, ln1_b,
           w_xq, w_xkv, w_xo, ln2_g, ln2_b, w_pq, sub_keys, w_down, w_up, ln3_g, ln3_b):
    depth = w_in.shape[0]
    bsz, seq, d = x.shape
    assert d == SUBLANES * LANES and seq % CHUNK == 0
    alpha = np.float32((2 * depth) ** 0.25)
    n_tok = bsz * seq
    ts = _tile(seq, 512)
    for l in range(depth):
        x = _mixer(x, w_in[l], b_in[l], sgu_g[l], sgu_b[l], w_spatial[l], b_spatial[l], conv_w[l],
                   w_mix_out[l], ln1_g[l], ln1_b[l], alpha, ts)
        kv = _kv_proj(mem, w_xkv[l])
        x = _xattn(x, kv, w_xq[l], w_xo[l], ln2_g[l], ln2_b[l], alpha, ts)
        xt = x.reshape(n_tok, d)
        idx, gate = _peer_route(xt, w_pq[l], sub_keys[l], _tile(n_tok, 256))
        tb = _tile(n_tok, 512)
        pack = _tile(w_down.shape[1], 512)
        c = _peer_down(idx, xt, gate, _pack_rows(w_down[l], pack), tb)
        x = _peer_up(idx, c, xt, _pack_rows(w_up[l], pack), ln3_g[l], ln3_b[l], alpha, tb)
        x = x.reshape(bsz, seq, d)
    return x
```

```python
import functools

import jax
import jax.numpy as jnp
import numpy as np
from jax import lax
from jax.experimental import pallas as pl
from jax.experimental.pallas import tpu as pltpu

LN_EPS = 1e-5
LANES = 128
SUBLANES = 8
A_HEADS = 4
CHUNK = 128
X_HEADS = 4
PEER_HEADS = 8
N_KEYS = 128
TOPK = 16
ROW_WORDS = 4
VMEM_LIMIT = 56 * 1024 * 1024

F32 = jnp.float32
BF16 = jnp.bfloat16


def _gelu(x):
    return 0.5 * x * (1.0 + lax.erf(x * np.float32(2.0 ** -0.5)))


def _ln(v, g, b):
    mu = jnp.mean(v, axis=-1, keepdims=True)
    d = v - mu
    var = jnp.mean(d * d, axis=-1, keepdims=True)
    return d * lax.rsqrt(var + LN_EPS) * g + b


def _dot(a, b):
    return jnp.dot(a, b, preferred_element_type=F32)


def _dot_nt(a, b):
    return lax.dot_general(a, b, (((1,), (1,)), ((), ())), preferred_element_type=F32)


def _mixer_kernel(x_ref, win_ref, bin_ref, sg_ref, sb_ref, ws_ref, bsp_ref, cw_ref, wout_ref,
                  lg_ref, lb_ref, o_ref, zc_ref, *, alpha, ts, aw):
    @pl.when(pl.program_id(1) == 0)
    def _():
        zc_ref[0:SUBLANES, :] = jnp.zeros((SUBLANES, zc_ref.shape[1]), F32)

    x = x_ref[0]
    xb = x.astype(BF16)

    def zcol(lo, hi):
        return _dot(xb, win_ref[:, lo:hi]) + bin_ref[:, lo:hi]

    hd = aw // A_HEADS
    u = _gelu(zcol(0, aw))
    v = _gelu(zcol(aw, 2 * aw))
    mixed = []
    for h in range(A_HEADS):
        sl = slice(h * hd, (h + 1) * hd)
        vh = _ln(v[:, sl], sg_ref[:, sl], sb_ref[:, sl]).astype(BF16)
        mixed.append(jnp.concatenate(
            [_dot(ws_ref[h], vh[c * CHUNK:(c + 1) * CHUNK]) + bsp_ref[:, sl]
             for c in range(ts // CHUNK)], axis=0))
    ya = u * jnp.concatenate(mixed, axis=-1)

    bw = zc_ref.shape[1]
    hb = zcol(2 * aw, 2 * aw + bw)
    gb = zcol(2 * aw + bw, 2 * aw + 2 * bw)
    gc = zcol(2 * aw + 2 * bw, 2 * aw + 3 * bw)
    zc_ref[SUBLANES:SUBLANES + ts, :] = gc * hb
    conv = (cw_ref[0:1, :] * zc_ref[SUBLANES - 2:SUBLANES - 2 + ts, :]
            + cw_ref[1:2, :] * zc_ref[SUBLANES - 1:SUBLANES - 1 + ts, :]
            + cw_ref[2:3, :] * zc_ref[SUBLANES:SUBLANES + ts, :])
    yb = gb * conv
    zc_ref[0:SUBLANES, :] = zc_ref[ts:ts + SUBLANES, :]

    h = _dot(ya.astype(BF16), wout_ref[0:aw, :]) + _dot(yb.astype(BF16), wout_ref[aw:, :])
    o_ref[0] = _ln(alpha * x + h, lg_ref[...], lb_ref[...])


def _mixer(x, w_in, b_in, sgu_g, sgu_b, w_spatial, b_spatial, conv_w, w_out, ln_g, ln_b, alpha, ts):
    bsz, seq, d = x.shape
    aw = A_HEADS * sgu_g.shape[-1]
    bw = conv_w.shape[-1]
    causal = jnp.tril(jnp.ones((CHUNK, CHUNK), dtype=bool))
    ws = jnp.where(causal[None], w_spatial, 0.0).astype(BF16)
    bsp = jnp.repeat(b_spatial.T, aw // A_HEADS, axis=1)
    const = lambda *shape: pl.BlockSpec(shape, lambda b, i: (0,) * len(shape))
    return pl.pallas_call(
        functools.partial(_mixer_kernel, alpha=alpha, ts=ts, aw=aw),
        grid=(bsz, seq // ts),
        in_specs=[
            pl.BlockSpec((1, ts, d), lambda b, i: (b, i, 0)),
            const(d, w_in.shape[1]), const(1, w_in.shape[1]), const(1, aw), const(1, aw),
            const(A_HEADS, CHUNK, CHUNK), const(CHUNK, aw), const(conv_w.shape[0], bw),
            const(aw + bw, d), const(1, d), const(1, d),
        ],
        out_specs=pl.BlockSpec((1, ts, d), lambda b, i: (b, i, 0)),
        out_shape=jax.ShapeDtypeStruct(x.shape, F32),
        scratch_shapes=[pltpu.VMEM((ts + 2 * SUBLANES, bw), F32)],
        compiler_params=pltpu.CompilerParams(
            dimension_semantics=("parallel", "arbitrary"), vmem_limit_bytes=VMEM_LIMIT),
        name="mixer",
    )(x, w_in.astype(BF16), b_in.reshape(1, -1), sgu_g.reshape(1, -1), sgu_b.reshape(1, -1),
      ws, bsp, conv_w.reshape(conv_w.shape[0], bw), w_out.astype(BF16),
      ln_g.reshape(1, -1), ln_b.reshape(1, -1))


def _kv_kernel(m_ref, w_ref, o_ref):
    o_ref[0] = _dot(m_ref[0].astype(BF16), w_ref[...]).astype(BF16)


def _kv_proj(mem, w_kv):
    bsz, m, d = mem.shape
    n = w_kv.shape[1]
    return pl.pallas_call(
        _kv_kernel,
        grid=(bsz,),
        in_specs=[pl.BlockSpec((1, m, d), lambda b: (b, 0, 0)),
                  pl.BlockSpec((d, n), lambda b: (0, 0))],
        out_specs=pl.BlockSpec((1, m, n), lambda b: (b, 0, 0)),
        out_shape=jax.ShapeDtypeStruct((bsz, m, n), BF16),
        compiler_params=pltpu.CompilerParams(
            dimension_semantics=("parallel",), vmem_limit_bytes=VMEM_LIMIT),
        name="kv_proj",
    )(mem, w_kv.astype(BF16))


def _xattn_kernel(x_ref, kv_ref, wq_ref, wo_ref, lg_ref, lb_ref, o_ref, *, alpha):
    x = x_ref[0]
    d = x.shape[-1]
    dk = d // X_HEADS
    q = _dot(x.astype(BF16), wq_ref[...])
    outs = []
    for h in range(X_HEADS):
        qh = q[:, h * dk:(h + 1) * dk].astype(BF16)
        s = _dot_nt(qh, kv_ref[0, :, h * dk:(h + 1) * dk]) * np.float32(dk ** -0.5)
        e = jnp.exp(s - jnp.max(s, axis=-1, keepdims=True))
        p = e / jnp.sum(e, axis=-1, keepdims=True)
        outs.append(_dot(p.astype(BF16), kv_ref[0, :, d + h * dk:d + (h + 1) * dk]))
    o = jnp.concatenate(outs, axis=-1).astype(BF16)
    o_ref[0] = _ln(alpha * x + _dot(o, wo_ref[...]), lg_ref[...], lb_ref[...])


def _xattn(x, kv, w_q, w_o, ln_g, ln_b, alpha, ts):
    bsz, seq, d = x.shape
    m = kv.shape[1]
    return pl.pallas_call(
        functools.partial(_xattn_kernel, alpha=alpha),
        grid=(bsz, seq // ts),
        in_specs=[
            pl.BlockSpec((1, ts, d), lambda b, i: (b, i, 0)),
            pl.BlockSpec((1, m, 2 * d), lambda b, i: (b, 0, 0)),
            pl.BlockSpec((d, d), lambda b, i: (0, 0)),
            pl.BlockSpec((d, d), lambda b, i: (0, 0)),
            pl.BlockSpec((1, d), lambda b, i: (0, 0)),
            pl.BlockSpec((1, d), lambda b, i: (0, 0)),
        ],
        out_specs=pl.BlockSpec((1, ts, d), lambda b, i: (b, i, 0)),
        out_shape=jax.ShapeDtypeStruct(x.shape, F32),
        compiler_params=pltpu.CompilerParams(
            dimension_semantics=("parallel", "parallel"), vmem_limit_bytes=VMEM_LIMIT),
        name="xattn",
    )(x, kv, w_q.astype(BF16), w_o.astype(BF16), ln_g.reshape(1, -1), ln_b.reshape(1, -1))


def _extract_top(vals, tags, payload, rounds):
    tops, picks = [], []
    for _ in range(rounds):
        m = jnp.max(vals, axis=0, keepdims=True)
        first = jnp.min(jnp.where(vals == m, tags, np.float32(2 ** 24)), axis=0, keepdims=True)
        sel = tags == first
        tops.append(m)
        if payload is None:
            picks.append(first)
        else:
            picks.append(jnp.max(jnp.where(sel, payload, -1.0), axis=0, keepdims=True))
        vals = jnp.where(sel, -jnp.inf, vals)
    return jnp.concatenate(tops, axis=0), jnp.concatenate(picks, axis=0)


def _route_kernel(x_ref, wq_ref, keys_ref, code_ref, idx_ref, gate_ref, idx_t, gate_t, sc_ref, top_ref):
    tt = x_ref.shape[0]
    xb = x_ref[...].astype(BF16)
    row = lax.broadcasted_iota(jnp.int32, (N_KEYS, tt), 0).astype(F32)
    code = code_ref[...]

    q = _dot(xb, wq_ref[...]).astype(BF16)
    hk = q.shape[1] // (2 * PEER_HEADS)
    for i in range(2 * PEER_HEADS):
        sc_ref[i] = _dot_nt(keys_ref[i // 2, i % 2], q[:, i * hk:(i + 1) * hk])
    top_ref[1] = jnp.zeros(top_ref.shape[1:], F32)

    def sub_top(h, slot):
        va, ia = _extract_top(sc_ref[2 * h], row, None, TOPK)
        vb, ib = _extract_top(sc_ref[2 * h + 1], row, None, TOPK)
        top_ref[slot] = jnp.concatenate([va, vb, ia, ib], axis=0)

    def pair_top(h, slot):
        t = top_ref[slot]
        va, vb, ia, ib = (t[i * TOPK:(i + 1) * TOPK] for i in range(4))
        ea = ia * np.float32(N_KEYS)
        cand = [va + vb[0:1], ]
        eid = [ea + ib[0:1], ]
        for j in range(1, SUBLANES):
            cand.append(va[0:SUBLANES] + vb[j:j + 1])
            eid.append(ea[0:SUBLANES] + ib[j:j + 1])
        cand.append(va[0:1] + vb[SUBLANES:])
        eid.append(ea[0:1] + ib[SUBLANES:])
        cand = jnp.where(code >= 0, jnp.concatenate(cand, axis=0), -jnp.inf)
        eid = jnp.concatenate(eid, axis=0)
        top, expert = _extract_top(cand, code, eid, TOPK)
        e = jnp.exp(top - top[0:1])
        r0 = pl.multiple_of(h * TOPK, TOPK)
        gate_t[pl.ds(r0, TOPK), :] = e / jnp.sum(e, axis=0, keepdims=True)
        idx_t[pl.ds(r0, TOPK), :] = expert.astype(jnp.int32) * ROW_WORDS

    def two_heads(p, carry):
        h = 2 * p
        sub_top(h, 0)
        pair_top(jnp.maximum(h - 1, 0), 1)
        sub_top(h + 1, 1)
        pair_top(h, 0)
        return carry

    lax.fori_loop(0, PEER_HEADS // 2, two_heads, 0)
    pair_top(PEER_HEADS - 1, 1)
    idx_ref[...] = idx_t[...].T
    gate_ref[...] = gate_t[...].T


def _candidate_codes(tt):
    codes = [[i * TOPK for i in range(TOPK)]]
    for j in range(1, SUBLANES):
        codes.append([i * TOPK + j if (i + 1) * (j + 1) <= TOPK else -1 for i in range(SUBLANES)])
    codes.append(list(range(SUBLANES, TOPK)))
    flat = np.asarray([c for blk in codes for c in blk], dtype=np.float32)
    return jnp.asarray(np.broadcast_to(flat[:, None], (flat.shape[0], tt)))


def _peer_route(xt, w_query, sub_keys, tt):
    n_tok, d = xt.shape
    wq = w_query.astype(BF16)
    hk = PEER_HEADS * TOPK
    code = _candidate_codes(tt)
    return pl.pallas_call(
        _route_kernel,
        grid=(n_tok // tt,),
        in_specs=[
            pl.BlockSpec((tt, d), lambda i: (i, 0)),
            pl.BlockSpec(wq.shape, lambda i: (0, 0)),
            pl.BlockSpec(sub_keys.shape, lambda i: (0, 0, 0, 0)),
            pl.BlockSpec(code.shape, lambda i: (0, 0)),
        ],
        out_specs=[pl.BlockSpec((tt, hk), lambda i: (i, 0)),
                   pl.BlockSpec((tt, hk), lambda i: (i, 0))],
        out_shape=[jax.ShapeDtypeStruct((n_tok, hk), jnp.int32),
                   jax.ShapeDtypeStruct((n_tok, hk), F32)],
        scratch_shapes=[pltpu.VMEM((hk, tt), jnp.int32), pltpu.VMEM((hk, tt), F32),
                        pltpu.VMEM((2 * PEER_HEADS, N_KEYS, tt), F32), pltpu.VMEM((2, 4 * TOPK, tt), F32)],
        compiler_params=pltpu.CompilerParams(
            dimension_semantics=("parallel",), vmem_limit_bytes=VMEM_LIMIT),
        name="peer_route",
    )(xt, wq, sub_keys.astype(BF16), code)


TOK_GROUP = 64


def _pack_kernel(w_ref, o_ref):
    n = w_ref.shape[0]
    for s in range(ROW_WORDS):
        lo = w_ref[:, 2 * s * LANES:(2 * s + 1) * LANES].astype(BF16).astype(F32)
        hi = w_ref[:, (2 * s + 1) * LANES:(2 * s + 2) * LANES].astype(BF16).astype(F32)
        word = pltpu.bitcast(hi, jnp.uint32) | (pltpu.bitcast(lo, jnp.uint32) >> 16)
        o_ref[pl.ds(s, n, stride=ROW_WORDS), :] = word


def _pack_rows(w, rows):
    n, d = w.shape
    assert d == 2 * ROW_WORDS * LANES and n % rows == 0
    return pl.pallas_call(
        _pack_kernel,
        grid=(n // rows,),
        in_specs=[pl.BlockSpec((rows, d), lambda i: (i, 0))],
        out_specs=pl.BlockSpec((rows * ROW_WORDS, LANES), lambda i: (i, 0)),
        out_shape=jax.ShapeDtypeStruct((n * ROW_WORDS, LANES), jnp.uint32),
        compiler_params=pltpu.CompilerParams(
            dimension_semantics=("parallel",), vmem_limit_bytes=VMEM_LIMIT),
        name="pack_rows",
    )(w)


def _gather_rows(idx_ref, t, tab_ref, hk):
    words = [tab_ref[pl.ds(pl.multiple_of(idx_ref[t, k], ROW_WORDS), ROW_WORDS), :] for k in range(hk)]
    return pltpu.bitcast(jnp.concatenate(words, axis=0), BF16)


def _diag_mask(hk):
    r = lax.broadcasted_iota(jnp.int32, (SUBLANES, hk * SUBLANES), 0)
    c = lax.broadcasted_iota(jnp.int32, (SUBLANES, hk * SUBLANES), 1)
    return (c & (SUBLANES - 1)) == r


def _idx_copy(idx_hbm, row0, buf, sem):
    return pltpu.make_async_copy(idx_hbm.at[pl.ds(row0, TOK_GROUP)], buf, sem)


def _for_each_group(idx_hbm, tb, idx_a, idx_b, sems, process):
    step = pl.program_id(0)
    base = step * tb
    last = idx_hbm.shape[0] - TOK_GROUP

    @pl.when(step == 0)
    def _():
        _idx_copy(idx_hbm, 0, idx_a, sems.at[0]).start()

    def pair(i, carry):
        t0 = pl.multiple_of(i * (2 * TOK_GROUP), 2 * TOK_GROUP)
        _idx_copy(idx_hbm, base + t0 + TOK_GROUP, idx_b, sems.at[1]).start()
        _idx_copy(idx_hbm, 0, idx_a, sems.at[0]).wait()
        process(idx_a, t0, 0)
        _idx_copy(idx_hbm, jnp.minimum(base + t0 + 2 * TOK_GROUP, last), idx_a, sems.at[0]).start()
        _idx_copy(idx_hbm, 0, idx_b, sems.at[1]).wait()
        process(idx_b, t0 + TOK_GROUP, 1)
        return carry

    lax.fori_loop(0, tb // (2 * TOK_GROUP), pair, 0)

    @pl.when(step == pl.num_programs(0) - 1)
    def _():
        _idx_copy(idx_hbm, 0, idx_a, sems.at[0]).wait()


def _idx_scratch(hk):
    return [pltpu.SMEM((TOK_GROUP, hk), jnp.int32), pltpu.SMEM((TOK_GROUP, hk), jnp.int32),
            pltpu.SemaphoreType.DMA((2,))]


def _down_kernel(idx_hbm, x_ref, gate_ref, fold_ref, tab_ref, c_ref, idx_a, idx_b, sems, xt_ref,
                 part_a, part_b):
    hk = gate_ref.shape[1]
    tb = x_ref.shape[0]
    mask = _diag_mask(hk)
    parts = (part_a, part_b)

    def finish(part_ref, t0):
        z = _dot(part_ref[...].astype(BF16), fold_ref[...])
        act = jnp.concatenate(
            [jnp.sum(z[j * SUBLANES:(j + 1) * SUBLANES], axis=0, keepdims=True)
             for j in range(TOK_GROUP)], axis=0)
        c_ref[pl.ds(t0, TOK_GROUP), :] = gate_ref[pl.ds(t0, TOK_GROUP), :] * _gelu(act)

    def process(idx_ref, t0, slot):
        finish(parts[1 - slot], pl.multiple_of(jnp.maximum(t0 - TOK_GROUP, 0), TOK_GROUP))
        xs = x_ref[pl.ds(t0, TOK_GROUP), :]
        for r in range(SUBLANES):
            xt_ref[pl.ds(r, TOK_GROUP, stride=SUBLANES), :] = xs[:, r * LANES:(r + 1) * LANES]
        for j in range(TOK_GROUP):
            rows = _gather_rows(idx_ref, j, tab_ref, hk)
            xj = xt_ref[j * SUBLANES:(j + 1) * SUBLANES, :].astype(BF16)
            s = _dot_nt(xj, rows)
            parts[slot][j * SUBLANES:(j + 1) * SUBLANES, :] = jnp.where(mask, s, 0.0)

    part_b[...] = jnp.zeros(part_b.shape, F32)
    _for_each_group(idx_hbm, tb, idx_a, idx_b, sems, process)
    finish(part_b, tb - TOK_GROUP)


def _peer_down(idx, xt, gate, tab, tb):
    n_tok, hk = idx.shape
    assert tb % (2 * TOK_GROUP) == 0 and xt.shape[1] == SUBLANES * LANES
    fold = (jnp.arange(hk * SUBLANES)[:, None] // SUBLANES == jnp.arange(hk)[None, :]).astype(BF16)
    return pl.pallas_call(
        _down_kernel,
        grid=(n_tok // tb,),
        in_specs=[
            pl.BlockSpec(memory_space=pl.ANY),
            pl.BlockSpec((tb, SUBLANES * LANES), lambda i: (i, 0)),
            pl.BlockSpec((tb, hk), lambda i: (i, 0)),
            pl.BlockSpec(fold.shape, lambda i: (0, 0)),
            pl.BlockSpec(memory_space=pltpu.VMEM),
        ],
        out_specs=pl.BlockSpec((tb, hk), lambda i: (i, 0)),
        out_shape=jax.ShapeDtypeStruct((n_tok, hk), F32),
        scratch_shapes=_idx_scratch(hk) + [pltpu.VMEM((TOK_GROUP * SUBLANES, LANES), F32)]
        + [pltpu.VMEM((TOK_GROUP * SUBLANES, hk * SUBLANES), F32) for _ in range(2)],
        compiler_params=pltpu.CompilerParams(
            dimension_semantics=("arbitrary",), vmem_limit_bytes=VMEM_LIMIT),
        name="peer_down",
    )(idx, xt, gate, fold, tab)


def _up_kernel(idx_hbm, c_ref, x_ref, spread_ref, tab_ref, g_ref, b_ref, o_ref, idx_a, idx_b, sems,
               ht_a, ht_b, *, alpha):
    hk = c_ref.shape[1]
    tb = c_ref.shape[0]
    mask = _diag_mask(hk)
    hts = (ht_a, ht_b)

    def finish(ht_ref, t0):
        h = jnp.concatenate([ht_ref[pl.ds(r, TOK_GROUP, stride=SUBLANES), :] for r in range(SUBLANES)],
                            axis=-1)
        o_ref[pl.ds(t0, TOK_GROUP), :] = _ln(alpha * x_ref[pl.ds(t0, TOK_GROUP), :] + h,
                                              g_ref[...], b_ref[...])

    def process(idx_ref, t0, slot):
        finish(hts[1 - slot], pl.multiple_of(jnp.maximum(t0 - TOK_GROUP, 0), TOK_GROUP))
        cexp = _dot(c_ref[pl.ds(t0, TOK_GROUP), :].astype(BF16), spread_ref[...])
        for j in range(TOK_GROUP):
            rows = _gather_rows(idx_ref, j, tab_ref, hk)
            cj = jnp.where(mask, jnp.broadcast_to(cexp[j:j + 1, :], mask.shape), 0.0)
            hts[slot][j * SUBLANES:(j + 1) * SUBLANES, :] = _dot(cj.astype(BF16), rows)

    ht_b[...] = jnp.zeros(ht_b.shape, F32)
    _for_each_group(idx_hbm, tb, idx_a, idx_b, sems, process)
    finish(ht_b, tb - TOK_GROUP)


def _peer_up(idx, c, xt, tab, ln_g, ln_b, alpha, tb):
    n_tok, hk = idx.shape
    d = xt.shape[1]
    assert tb % (2 * TOK_GROUP) == 0 and d == SUBLANES * LANES
    spread = (jnp.arange(hk)[:, None] == jnp.arange(hk * SUBLANES)[None, :] // SUBLANES).astype(BF16)
    return pl.pallas_call(
        functools.partial(_up_kernel, alpha=alpha),
        grid=(n_tok // tb,),
        in_specs=[
            pl.BlockSpec(memory_space=pl.ANY),
            pl.BlockSpec((tb, hk), lambda i: (i, 0)),
            pl.BlockSpec((tb, d), lambda i: (i, 0)),
            pl.BlockSpec(spread.shape, lambda i: (0, 0)),
            pl.BlockSpec(memory_space=pltpu.VMEM),
            pl.BlockSpec((1, d), lambda i: (0, 0)),
            pl.BlockSpec((1, d), lambda i: (0, 0)),
        ],
        out_specs=pl.BlockSpec((tb, d), lambda i: (i, 0)),
        out_shape=jax.ShapeDtypeStruct((n_tok, d), F32),
        scratch_shapes=_idx_scratch(hk)
        + [pltpu.VMEM((TOK_GROUP * SUBLANES, LANES), F32) for _ in range(2)],
        compiler_params=pltpu.CompilerParams(
            dimension_semantics=("arbitrary",), vmem_limit_bytes=VMEM_LIMIT),
        name="peer_up",
    )(idx, c, xt, spread, tab, ln_g.reshape(1, -1), ln_b.reshape(1, -1))


def _tile(n, want):
    return want if n % want == 0 else n


def kernel(x, mem, w_in, b_in, sgu_g, sgu_b, w_spatial, b_spatial, conv_w, w_mix_out, ln1_g, ln1_b,
           w_xq, w_xkv, w_xo, ln2_g, ln2_b, w_pq, sub_keys, w_down, w_up, ln3_g, ln3_b):
    depth = w_in.shape[0]
    bsz, seq, d = x.shape
    assert d == SUBLANES * LANES and seq % CHUNK == 0
    alpha = np.float32((2 * depth) ** 0.25)
    n_tok = bsz * seq
    ts = _tile(seq, 512)
    for l in range(depth):
        x = _mixer(x, w_in[l], b_in[l], sgu_g[l], sgu_b[l], w_spatial[l], b_spatial[l], conv_w[l],
                   w_mix_out[l], ln1_g[l], ln1_b[l], alpha, ts)
        kv = _kv_proj(mem, w_xkv[l])
        x = _xattn(x, kv, w_xq[l], w_xo[l], ln2_g[l], ln2_b[l], alpha, ts)
        xt = x.reshape(n_tok, d)
        idx, gate = _peer_route(xt, w_pq[l], sub_keys[l], _tile(n_tok, 512))
        tb = _tile(n_tok, 512)
        pack = _tile(w_down.shape[1], 512)
        c = _peer_down(idx, xt, gate, _pack_rows(w_down[l], pack), tb)
        x = _peer_up(idx, c, xt, _pack_rows(w_up[l], pack), ln3_g[l], ln3_b[l], alpha, tb)
        x = x.reshape(bsz, seq, d)
    return x
```

```python
import functools

import jax
import jax.numpy as jnp
import numpy as np
from jax import lax
from jax.experimental import pallas as pl
from jax.experimental.pallas import tpu as pltpu

LN_EPS = 1e-5
LANES = 128
SUBLANES = 8
A_HEADS = 4
CHUNK = 128
X_HEADS = 4
PEER_HEADS = 8
N_KEYS = 128
TOPK = 16
ROW_WORDS = 4
VMEM_LIMIT = 56 * 1024 * 1024

F32 = jnp.float32
BF16 = jnp.bfloat16


def _gelu(x):
    return 0.5 * x * (1.0 + lax.erf(x * np.float32(2.0 ** -0.5)))


def _ln(v, g, b):
    mu = jnp.mean(v, axis=-1, keepdims=True)
    d = v - mu
    var = jnp.mean(d * d, axis=-1, keepdims=True)
    return d * lax.rsqrt(var + LN_EPS) * g + b


def _dot(a, b):
    return jnp.dot(a, b, preferred_element_type=F32)


def _dot_nt(a, b):
    return lax.dot_general(a, b, (((1,), (1,)), ((), ())), preferred_element_type=F32)


def _mixer_kernel(x_ref, win_ref, bin_ref, sg_ref, sb_ref, ws_ref, bsp_ref, cw_ref, wout_ref,
                  lg_ref, lb_ref, o_ref, zc_ref, *, alpha, ts, aw):
    @pl.when(pl.program_id(1) == 0)
    def _():
        zc_ref[0:SUBLANES, :] = jnp.zeros((SUBLANES, zc_ref.shape[1]), F32)

    x = x_ref[0]
    xb = x.astype(BF16)

    def zcol(lo, hi):
        return _dot(xb, win_ref[:, lo:hi]) + bin_ref[:, lo:hi]

    hd = aw // A_HEADS
    u = _gelu(zcol(0, aw))
    v = _gelu(zcol(aw, 2 * aw))
    mixed = []
    for h in range(A_HEADS):
        sl = slice(h * hd, (h + 1) * hd)
        vh = _ln(v[:, sl], sg_ref[:, sl], sb_ref[:, sl]).astype(BF16)
        mixed.append(jnp.concatenate(
            [_dot(ws_ref[h], vh[c * CHUNK:(c + 1) * CHUNK]) + bsp_ref[:, sl]
             for c in range(ts // CHUNK)], axis=0))
    ya = u * jnp.concatenate(mixed, axis=-1)

    bw = zc_ref.shape[1]
    hb = zcol(2 * aw, 2 * aw + bw)
    gb = zcol(2 * aw + bw, 2 * aw + 2 * bw)
    gc = zcol(2 * aw + 2 * bw, 2 * aw + 3 * bw)
    zc_ref[SUBLANES:SUBLANES + ts, :] = gc * hb
    conv = (cw_ref[0:1, :] * zc_ref[SUBLANES - 2:SUBLANES - 2 + ts, :]
            + cw_ref[1:2, :] * zc_ref[SUBLANES - 1:SUBLANES - 1 + ts, :]
            + cw_ref[2:3, :] * zc_ref[SUBLANES:SUBLANES + ts, :])
    yb = gb * conv
    zc_ref[0:SUBLANES, :] = zc_ref[ts:ts + SUBLANES, :]

    h = _dot(ya.astype(BF16), wout_ref[0:aw, :]) + _dot(yb.astype(BF16), wout_ref[aw:, :])
    o_ref[0] = _ln(alpha * x + h, lg_ref[...], lb_ref[...])


def _mixer(x, w_in, b_in, sgu_g, sgu_b, w_spatial, b_spatial, conv_w, w_out, ln_g, ln_b, alpha, ts):
    bsz, seq, d = x.shape
    aw = A_HEADS * sgu_g.shape[-1]
    bw = conv_w.shape[-1]
    causal = jnp.tril(jnp.ones((CHUNK, CHUNK), dtype=bool))
    ws = jnp.where(causal[None], w_spatial, 0.0).astype(BF16)
    bsp = jnp.repeat(b_spatial.T, aw // A_HEADS, axis=1)
    const = lambda *shape: pl.BlockSpec(shape, lambda b, i: (0,) * len(shape))
    return pl.pallas_call(
        functools.partial(_mixer_kernel, alpha=alpha, ts=ts, aw=aw),
        grid=(bsz, seq // ts),
        in_specs=[
            pl.BlockSpec((1, ts, d), lambda b, i: (b, i, 0)),
            const(d, w_in.shape[1]), const(1, w_in.shape[1]), const(1, aw), const(1, aw),
            const(A_HEADS, CHUNK, CHUNK), const(CHUNK, aw), const(conv_w.shape[0], bw),
            const(aw + bw, d), const(1, d), const(1, d),
        ],
        out_specs=pl.BlockSpec((1, ts, d), lambda b, i: (b, i, 0)),
        out_shape=jax.ShapeDtypeStruct(x.shape, F32),
        scratch_shapes=[pltpu.VMEM((ts + 2 * SUBLANES, bw), F32)],
        compiler_params=pltpu.CompilerParams(
            dimension_semantics=("parallel", "arbitrary"), vmem_limit_bytes=VMEM_LIMIT),
        name="mixer",
    )(x, w_in.astype(BF16), b_in.reshape(1, -1), sgu_g.reshape(1, -1), sgu_b.reshape(1, -1),
      ws, bsp, conv_w.reshape(conv_w.shape[0], bw), w_out.astype(BF16),
      ln_g.reshape(1, -1), ln_b.reshape(1, -1))


def _kv_kernel(m_ref, w_ref, o_ref):
    o_ref[0] = _dot(m_ref[0].astype(BF16), w_ref[...]).astype(BF16)


def _kv_proj(mem, w_kv):
    bsz, m, d = mem.shape
    n = w_kv.shape[1]
    return pl.pallas_call(
        _kv_kernel,
        grid=(bsz,),
        in_specs=[pl.BlockSpec((1, m, d), lambda b: (b, 0, 0)),
                  pl.BlockSpec((d, n), lambda b: (0, 0))],
        out_specs=pl.BlockSpec((1, m, n), lambda b: (b, 0, 0)),
        out_shape=jax.ShapeDtypeStruct((bsz, m, n), BF16),
        compiler_params=pltpu.CompilerParams(
            dimension_semantics=("parallel",), vmem_limit_bytes=VMEM_LIMIT),
        name="kv_proj",
    )(mem, w_kv.astype(BF16))


def _xattn_kernel(x_ref, kv_ref, wq_ref, wo_ref, lg_ref, lb_ref, o_ref, *, alpha):
    x = x_ref[0]
    d = x.shape[-1]
    dk = d // X_HEADS
    q = _dot(x.astype(BF16), wq_ref[...])
    outs = []
    for h in range(X_HEADS):
        qh = q[:, h * dk:(h + 1) * dk].astype(BF16)
        s = _dot_nt(qh, kv_ref[0, :, h * dk:(h + 1) * dk]) * np.float32(dk ** -0.5)
        e = jnp.exp(s - jnp.max(s, axis=-1, keepdims=True))
        p = e / jnp.sum(e, axis=-1, keepdims=True)
        outs.append(_dot(p.astype(BF16), kv_ref[0, :, d + h * dk:d + (h + 1) * dk]))
    o = jnp.concatenate(outs, axis=-1).astype(BF16)
    o_ref[0] = _ln(alpha * x + _dot(o, wo_ref[...]), lg_ref[...], lb_ref[...])


def _xattn(x, kv, w_q, w_o, ln_g, ln_b, alpha, ts):
    bsz, seq, d = x.shape
    m = kv.shape[1]
    return pl.pallas_call(
        functools.partial(_xattn_kernel, alpha=alpha),
        grid=(bsz, seq // ts),
        in_specs=[
            pl.BlockSpec((1, ts, d), lambda b, i: (b, i, 0)),
            pl.BlockSpec((1, m, 2 * d), lambda b, i: (b, 0, 0)),
            pl.BlockSpec((d, d), lambda b, i: (0, 0)),
            pl.BlockSpec((d, d), lambda b, i: (0, 0)),
            pl.BlockSpec((1, d), lambda b, i: (0, 0)),
            pl.BlockSpec((1, d), lambda b, i: (0, 0)),
        ],
        out_specs=pl.BlockSpec((1, ts, d), lambda b, i: (b, i, 0)),
        out_shape=jax.ShapeDtypeStruct(x.shape, F32),
        compiler_params=pltpu.CompilerParams(
            dimension_semantics=("parallel", "parallel"), vmem_limit_bytes=VMEM_LIMIT),
        name="xattn",
    )(x, kv, w_q.astype(BF16), w_o.astype(BF16), ln_g.reshape(1, -1), ln_b.reshape(1, -1))


def _extract_top(vals, tags, payload, rounds):
    tops, picks = [], []
    for _ in range(rounds):
        m = jnp.max(vals, axis=0, keepdims=True)
        first = jnp.min(jnp.where(vals == m, tags, np.float32(2 ** 24)), axis=0, keepdims=True)
        sel = tags == first
        tops.append(m)
        picks.append(jnp.max(jnp.where(sel, payload, -1.0), axis=0, keepdims=True))
        vals = jnp.where(sel, -jnp.inf, vals)
    return jnp.concatenate(tops, axis=0), jnp.concatenate(picks, axis=0)


def _top_rows(vals, rounds):
    n, lanes = vals.shape
    nt = n // SUBLANES
    assert rounds <= nt
    sub = lax.broadcasted_iota(jnp.int32, (SUBLANES, lanes), 0).astype(F32)
    v = [vals[k * SUBLANES:(k + 1) * SUBLANES] for k in range(nt)]
    idx = [sub + np.float32(SUBLANES * k) for k in range(nt)]
    for p in range(nt):
        for k in range(p % 2, nt - 1, 2):
            gt = v[k + 1] > v[k]
            v[k], v[k + 1] = jnp.where(gt, v[k + 1], v[k]), jnp.where(gt, v[k], v[k + 1])
            idx[k], idx[k + 1] = jnp.where(gt, idx[k + 1], idx[k]), jnp.where(gt, idx[k], idx[k + 1])
    tops, picks = [], []
    for r in range(rounds):
        m = jnp.max(v[0], axis=0, keepdims=True)
        first = jnp.min(jnp.where(v[0] == m, idx[0], np.float32(2 ** 24)), axis=0, keepdims=True)
        tops.append(m)
        picks.append(first)
        hit = idx[0] == first
        for k in range(rounds - 1 - r):
            v[k] = jnp.where(hit, v[k + 1], v[k])
            idx[k] = jnp.where(hit, idx[k + 1], idx[k])
    return jnp.concatenate(tops, axis=0), jnp.concatenate(picks, axis=0)


def _route_kernel(x_ref, wq_ref, keys_ref, code_ref, idx_ref, gate_ref, idx_t, gate_t, sc_ref, top_ref):
    tt = x_ref.shape[0]
    xb = x_ref[...].astype(BF16)
    code = code_ref[...]

    q = _dot(xb, wq_ref[...]).astype(BF16)
    hk = q.shape[1] // (2 * PEER_HEADS)
    for i in range(2 * PEER_HEADS):
        sc_ref[i] = _dot_nt(keys_ref[i // 2, i % 2], q[:, i * hk:(i + 1) * hk])
    top_ref[1] = jnp.zeros(top_ref.shape[1:], F32)

    def sub_top(h, slot):
        va, ia = _top_rows(sc_ref[2 * h], TOPK)
        vb, ib = _top_rows(sc_ref[2 * h + 1], TOPK)
        top_ref[slot] = jnp.concatenate([va, vb, ia, ib], axis=0)

    def pair_top(h, slot):
        t = top_ref[slot]
        va, vb, ia, ib = (t[i * TOPK:(i + 1) * TOPK] for i in range(4))
        ea = ia * np.float32(N_KEYS)
        cand = [va + vb[0:1], ]
        eid = [ea + ib[0:1], ]
        for j in range(1, SUBLANES):
            cand.append(va[0:SUBLANES] + vb[j:j + 1])
            eid.append(ea[0:SUBLANES] + ib[j:j + 1])
        cand.append(va[0:1] + vb[SUBLANES:])
        eid.append(ea[0:1] + ib[SUBLANES:])
        cand = jnp.where(code >= 0, jnp.concatenate(cand, axis=0), -jnp.inf)
        eid = jnp.concatenate(eid, axis=0)
        top, expert = _extract_top(cand, code, eid, TOPK)
        e = jnp.exp(top - top[0:1])
        r0 = pl.multiple_of(h * TOPK, TOPK)
        gate_t[pl.ds(r0, TOPK), :] = e / jnp.sum(e, axis=0, keepdims=True)
        idx_t[pl.ds(r0, TOPK), :] = expert.astype(jnp.int32) * ROW_WORDS

    def two_heads(p, carry):
        h = 2 * p
        sub_top(h, 0)
        pair_top(jnp.maximum(h - 1, 0), 1)
        sub_top(h + 1, 1)
        pair_top(h, 0)
        return carry

    lax.fori_loop(0, PEER_HEADS // 2, two_heads, 0)
    pair_top(PEER_HEADS - 1, 1)
    idx_ref[...] = idx_t[...].T
    gate_ref[...] = gate_t[...].T


def _candidate_codes(tt):
    codes = [[i * TOPK for i in range(TOPK)]]
    for j in range(1, SUBLANES):
        codes.append([i * TOPK + j if (i + 1) * (j + 1) <= TOPK else -1 for i in range(SUBLANES)])
    codes.append(list(range(SUBLANES, TOPK)))
    flat = np.asarray([c for blk in codes for c in blk], dtype=np.float32)
    return jnp.asarray(np.broadcast_to(flat[:, None], (flat.shape[0], tt)))


def _peer_route(xt, w_query, sub_keys, tt):
    n_tok, d = xt.shape
    wq = w_query.astype(BF16)
    hk = PEER_HEADS * TOPK
    code = _candidate_codes(tt)
    return pl.pallas_call(
        _route_kernel,
        grid=(n_tok // tt,),
        in_specs=[
            pl.BlockSpec((tt, d), lambda i: (i, 0)),
            pl.BlockSpec(wq.shape, lambda i: (0, 0)),
            pl.BlockSpec(sub_keys.shape, lambda i: (0, 0, 0, 0)),
            pl.BlockSpec(code.shape, lambda i: (0, 0)),
        ],
        out_specs=[pl.BlockSpec((tt, hk), lambda i: (i, 0)),
                   pl.BlockSpec((tt, hk), lambda i: (i, 0))],
        out_shape=[jax.ShapeDtypeStruct((n_tok, hk), jnp.int32),
                   jax.ShapeDtypeStruct((n_tok, hk), F32)],
        scratch_shapes=[pltpu.VMEM((hk, tt), jnp.int32), pltpu.VMEM((hk, tt), F32),
                        pltpu.VMEM((2 * PEER_HEADS, N_KEYS, tt), F32), pltpu.VMEM((2, 4 * TOPK, tt), F32)],
        compiler_params=pltpu.CompilerParams(
            dimension_semantics=("parallel",), vmem_limit_bytes=VMEM_LIMIT),
        name="peer_route",
    )(xt, wq, sub_keys.astype(BF16), code)


TOK_GROUP = 64


def _pack_kernel(w_ref, o_ref):
    n = w_ref.shape[0]
    for s in range(ROW_WORDS):
        lo = w_ref[:, 2 * s * LANES:(2 * s + 1) * LANES].astype(BF16).astype(F32)
        hi = w_ref[:, (2 * s + 1) * LANES:(2 * s + 2) * LANES].astype(BF16).astype(F32)
        word = pltpu.bitcast(hi, jnp.uint32) | (pltpu.bitcast(lo, jnp.uint32) >> 16)
        o_ref[pl.ds(s, n, stride=ROW_WORDS), :] = word


def _pack_rows(w, rows):
    n, d = w.shape
    assert d == 2 * ROW_WORDS * LANES and n % rows == 0
    return pl.pallas_call(
        _pack_kernel,
        grid=(n // rows,),
        in_specs=[pl.BlockSpec((rows, d), lambda i: (i, 0))],
        out_specs=pl.BlockSpec((rows * ROW_WORDS, LANES), lambda i: (i, 0)),
        out_shape=jax.ShapeDtypeStruct((n * ROW_WORDS, LANES), jnp.uint32),
        compiler_params=pltpu.CompilerParams(
            dimension_semantics=("parallel",), vmem_limit_bytes=VMEM_LIMIT),
        name="pack_rows",
    )(w)


def _gather_rows(idx_ref, t, tab_ref, hk):
    words = [tab_ref[pl.ds(pl.multiple_of(idx_ref[t, k], ROW_WORDS), ROW_WORDS), :] for k in range(hk)]
    return pltpu.bitcast(jnp.concatenate(words, axis=0), BF16)


def _diag_mask(hk):
    r = lax.broadcasted_iota(jnp.int32, (SUBLANES, hk * SUBLANES), 0)
    c = lax.broadcasted_iota(jnp.int32, (SUBLANES, hk * SUBLANES), 1)
    return (c & (SUBLANES - 1)) == r


def _idx_copy(idx_hbm, row0, buf, sem):
    return pltpu.make_async_copy(idx_hbm.at[pl.ds(row0, TOK_GROUP)], buf, sem)


def _for_each_group(idx_hbm, tb, idx_a, idx_b, sems, process):
    step = pl.program_id(0)
    base = step * tb
    last = idx_hbm.shape[0] - TOK_GROUP

    @pl.when(step == 0)
    def _():
        _idx_copy(idx_hbm, 0, idx_a, sems.at[0]).start()

    def pair(i, carry):
        t0 = pl.multiple_of(i * (2 * TOK_GROUP), 2 * TOK_GROUP)
        _idx_copy(idx_hbm, base + t0 + TOK_GROUP, idx_b, sems.at[1]).start()
        _idx_copy(idx_hbm, 0, idx_a, sems.at[0]).wait()
        process(idx_a, t0, 0)
        _idx_copy(idx_hbm, jnp.minimum(base + t0 + 2 * TOK_GROUP, last), idx_a, sems.at[0]).start()
        _idx_copy(idx_hbm, 0, idx_b, sems.at[1]).wait()
        process(idx_b, t0 + TOK_GROUP, 1)
        return carry

    lax.fori_loop(0, tb // (2 * TOK_GROUP), pair, 0)

    @pl.when(step == pl.num_programs(0) - 1)
    def _():
        _idx_copy(idx_hbm, 0, idx_a, sems.at[0]).wait()


def _idx_scratch(hk):
    return [pltpu.SMEM((TOK_GROUP, hk), jnp.int32), pltpu.SMEM((TOK_GROUP, hk), jnp.int32),
            pltpu.SemaphoreType.DMA((2,))]


def _down_kernel(idx_hbm, x_ref, gate_ref, fold_ref, tab_ref, c_ref, idx_a, idx_b, sems, xt_ref,
                 part_a, part_b):
    hk = gate_ref.shape[1]
    tb = x_ref.shape[0]
    mask = _diag_mask(hk)
    parts = (part_a, part_b)

    def finish(part_ref, t0):
        z = _dot(part_ref[...].astype(BF16), fold_ref[...])
        act = jnp.concatenate(
            [jnp.sum(z[j * SUBLANES:(j + 1) * SUBLANES], axis=0, keepdims=True)
             for j in range(TOK_GROUP)], axis=0)
        c_ref[pl.ds(t0, TOK_GROUP), :] = gate_ref[pl.ds(t0, TOK_GROUP), :] * _gelu(act)

    def process(idx_ref, t0, slot):
        finish(parts[1 - slot], pl.multiple_of(jnp.maximum(t0 - TOK_GROUP, 0), TOK_GROUP))
        xs = x_ref[pl.ds(t0, TOK_GROUP), :]
        for r in range(SUBLANES):
            xt_ref[pl.ds(r, TOK_GROUP, stride=SUBLANES), :] = xs[:, r * LANES:(r + 1) * LANES]
        for j in range(TOK_GROUP):
            rows = _gather_rows(idx_ref, j, tab_ref, hk)
            xj = xt_ref[j * SUBLANES:(j + 1) * SUBLANES, :].astype(BF16)
            s = _dot_nt(xj, rows)
            parts[slot][j * SUBLANES:(j + 1) * SUBLANES, :] = jnp.where(mask, s, 0.0)

    part_b[...] = jnp.zeros(part_b.shape, F32)
    _for_each_group(idx_hbm, tb, idx_a, idx_b, sems, process)
    finish(part_b, tb - TOK_GROUP)


def _peer_down(idx, xt, gate, tab, tb):
    n_tok, hk = idx.shape
    assert tb % (2 * TOK_GROUP) == 0 and xt.shape[1] == SUBLANES * LANES
    fold = (jnp.arange(hk * SUBLANES)[:, None] // SUBLANES == jnp.arange(hk)[None, :]).astype(BF16)
    return pl.pallas_call(
        _down_kernel,
        grid=(n_tok // tb,),
        in_specs=[
            pl.BlockSpec(memory_space=pl.ANY),
            pl.BlockSpec((tb, SUBLANES * LANES), lambda i: (i, 0)),
            pl.BlockSpec((tb, hk), lambda i: (i, 0)),
            pl.BlockSpec(fold.shape, lambda i: (0, 0)),
            pl.BlockSpec(memory_space=pltpu.VMEM),
        ],
        out_specs=pl.BlockSpec((tb, hk), lambda i: (i, 0)),
        out_shape=jax.ShapeDtypeStruct((n_tok, hk), F32),
        scratch_shapes=_idx_scratch(hk) + [pltpu.VMEM((TOK_GROUP * SUBLANES, LANES), F32)]
        + [pltpu.VMEM((TOK_GROUP * SUBLANES, hk * SUBLANES), F32) for _ in range(2)],
        compiler_params=pltpu.CompilerParams(
            dimension_semantics=("arbitrary",), vmem_limit_bytes=VMEM_LIMIT),
        name="peer_down",
    )(idx, xt, gate, fold, tab)


def _up_kernel(idx_hbm, c_ref, x_ref, spread_ref, tab_ref, g_ref, b_ref, o_ref, idx_a, idx_b, sems,
               ht_a, ht_b, *, alpha):
    hk = c_ref.shape[1]
    tb = c_ref.shape[0]
    mask = _diag_mask(hk)
    hts = (ht_a, ht_b)

    def finish(ht_ref, t0):
        h = jnp.concatenate([ht_ref[pl.ds(r, TOK_GROUP, stride=SUBLANES), :] for r in range(SUBLANES)],
                            axis=-1)
        o_ref[pl.ds(t0, TOK_GROUP), :] = _ln(alpha * x_ref[pl.ds(t0, TOK_GROUP), :] + h,
                                              g_ref[...], b_ref[...])

    def process(idx_ref, t0, slot):
        finish(hts[1 - slot], pl.multiple_of(jnp.maximum(t0 - TOK_GROUP, 0), TOK_GROUP))
        cexp = _dot(c_ref[pl.ds(t0, TOK_GROUP), :].astype(BF16), spread_ref[...])
        for j in range(TOK_GROUP):
            rows = _gather_rows(idx_ref, j, tab_ref, hk)
            cj = jnp.where(mask, jnp.broadcast_to(cexp[j:j + 1, :], mask.shape), 0.0)
            hts[slot][j * SUBLANES:(j + 1) * SUBLANES, :] = _dot(cj.astype(BF16), rows)

    ht_b[...] = jnp.zeros(ht_b.shape, F32)
    _for_each_group(idx_hbm, tb, idx_a, idx_b, sems, process)
    finish(ht_b, tb - TOK_GROUP)


def _peer_up(idx, c, xt, tab, ln_g, ln_b, alpha, tb):
    n_tok, hk = idx.shape
    d = xt.shape[1]
    assert tb % (2 * TOK_GROUP) == 0 and d == SUBLANES * LANES
    spread = (jnp.arange(hk)[:, None] == jnp.arange(hk * SUBLANES)[None, :] // SUBLANES).astype(BF16)
    return pl.pallas_call(
        functools.partial(_up_kernel, alpha=alpha),
        grid=(n_tok // tb,),
        in_specs=[
            pl.BlockSpec(memory_space=pl.ANY),
            pl.BlockSpec((tb, hk), lambda i: (i, 0)),
            pl.BlockSpec((tb, d), lambda i: (i, 0)),
            pl.BlockSpec(spread.shape, lambda i: (0, 0)),
            pl.BlockSpec(memory_space=pltpu.VMEM),
            pl.BlockSpec((1, d), lambda i: (0, 0)),
            pl.BlockSpec((1, d), lambda i: (0, 0)),
        ],
        out_specs=pl.BlockSpec((tb, d), lambda i: (i, 0)),
        out_shape=jax.ShapeDtypeStruct((n_tok, d), F32),
        scratch_shapes=_idx_scratch(hk)
        + [pltpu.VMEM((TOK_GROUP * SUBLANES, LANES), F32) for _ in range(2)],
        compiler_params=pltpu.CompilerParams(
            dimension_semantics=("arbitrary",), vmem_limit_bytes=VMEM_LIMIT),
        name="peer_up",
    )(idx, c, xt, spread, tab, ln_g.reshape(1, -1), ln_b.reshape(1, -1))


def _tile(n, want):
    return want if n % want == 0 else n


def kernel(x, mem, w_in, b_in, sgu_g, sgu_b, w_spatial, b_spatial, conv_w, w_mix_out, ln1_g, ln1_b,
           w_xq, w_xkv, w_xo, ln2_g, ln2_b, w_pq, sub_keys, w_down, w_up, ln3_g, ln3_b):
    depth = w_in.shape[0]
    bsz, seq, d = x.shape
    assert d == SUBLANES * LANES and seq % CHUNK == 0
    alpha = np.float32((2 * depth) ** 0.25)
    n_tok = bsz * seq
    ts = _tile(seq, 512)
    for l in range(depth):
        x = _mixer(x, w_in[l], b_in[l], sgu_g[l], sgu_b[l], w_spatial[l], b_spatial[l], conv_w[l],
                   w_mix_out[l], ln1_g[l], ln1_b[l], alpha, ts)
        kv = _kv_proj(mem, w_xkv[l])
        x = _xattn(x, kv, w_xq[l], w_xo[l], ln2_g[l], ln2_b[l], alpha, ts)
        xt = x.reshape(n_tok, d)
        idx, gate = _peer_route(xt, w_pq[l], sub_keys[l], _tile(n_tok, 512))
        tb = _tile(n_tok, 512)
        pack = _tile(w_down.shape[1], 512)
        c = _peer_down(idx, xt, gate, _pack_rows(w_down[l], pack), tb)
        x = _peer_up(idx, c, xt, _pack_rows(w_up[l], pack), ln3_g[l], ln3_b[l], alpha, tb)
        x = x.reshape(bsz, seq, d)
    return x
```

```python
import functools

import jax
import jax.numpy as jnp
import numpy as np
from jax import lax
from jax.experimental import pallas as pl
from jax.experimental.pallas import tpu as pltpu

LN_EPS = 1e-5
LANES = 128
SUBLANES = 8
A_HEADS = 4
CHUNK = 128
X_HEADS = 4
PEER_HEADS = 8
N_KEYS = 128
TOPK = 16
ROW_WORDS = 4
VMEM_LIMIT = 56 * 1024 * 1024

F32 = jnp.float32
BF16 = jnp.bfloat16


def _gelu(x):
    return 0.5 * x * (1.0 + lax.erf(x * np.float32(2.0 ** -0.5)))


def _ln(v, g, b):
    mu = jnp.mean(v, axis=-1, keepdims=True)
    d = v - mu
    var = jnp.mean(d * d, axis=-1, keepdims=True)
    return d * lax.rsqrt(var + LN_EPS) * g + b


def _dot(a, b):
    return jnp.dot(a, b, preferred_element_type=F32)


def _dot_nt(a, b):
    return lax.dot_general(a, b, (((1,), (1,)), ((), ())), preferred_element_type=F32)


def _mixer_kernel(x_ref, win_ref, bin_ref, sg_ref, sb_ref, ws_ref, bsp_ref, cw_ref, wout_ref,
                  lg_ref, lb_ref, o_ref, zc_ref, *, alpha, ts, aw):
    @pl.when(pl.program_id(1) == 0)
    def _():
        zc_ref[0:SUBLANES, :] = jnp.zeros((SUBLANES, zc_ref.shape[1]), F32)

    x = x_ref[0]
    xb = x.astype(BF16)

    def zcol(lo, hi):
        return _dot(xb, win_ref[:, lo:hi]) + bin_ref[:, lo:hi]

    hd = aw // A_HEADS
    u = _gelu(zcol(0, aw))
    v = _gelu(zcol(aw, 2 * aw))
    mixed = []
    for h in range(A_HEADS):
        sl = slice(h * hd, (h + 1) * hd)
        vh = _ln(v[:, sl], sg_ref[:, sl], sb_ref[:, sl]).astype(BF16)
        mixed.append(jnp.concatenate(
            [_dot(ws_ref[h], vh[c * CHUNK:(c + 1) * CHUNK]) + bsp_ref[:, sl]
             for c in range(ts // CHUNK)], axis=0))
    ya = u * jnp.concatenate(mixed, axis=-1)

    bw = zc_ref.shape[1]
    hb = zcol(2 * aw, 2 * aw + bw)
    gb = zcol(2 * aw + bw, 2 * aw + 2 * bw)
    gc = zcol(2 * aw + 2 * bw, 2 * aw + 3 * bw)
    zc_ref[SUBLANES:SUBLANES + ts, :] = gc * hb
    conv = (cw_ref[0:1, :] * zc_ref[SUBLANES - 2:SUBLANES - 2 + ts, :]
            + cw_ref[1:2, :] * zc_ref[SUBLANES - 1:SUBLANES - 1 + ts, :]
            + cw_ref[2:3, :] * zc_ref[SUBLANES:SUBLANES + ts, :])
    yb = gb * conv
    zc_ref[0:SUBLANES, :] = zc_ref[ts:ts + SUBLANES, :]

    h = _dot(ya.astype(BF16), wout_ref[0:aw, :]) + _dot(yb.astype(BF16), wout_ref[aw:, :])
    o_ref[0] = _ln(alpha * x + h, lg_ref[...], lb_ref[...])


def _mixer(x, w_in, b_in, sgu_g, sgu_b, w_spatial, b_spatial, conv_w, w_out, ln_g, ln_b, alpha, ts):
    bsz, seq, d = x.shape
    aw = A_HEADS * sgu_g.shape[-1]
    bw = conv_w.shape[-1]
    causal = jnp.tril(jnp.ones((CHUNK, CHUNK), dtype=bool))
    ws = jnp.where(causal[None], w_spatial, 0.0).astype(BF16)
    bsp = jnp.repeat(b_spatial.T, aw // A_HEADS, axis=1)
    const = lambda *shape: pl.BlockSpec(shape, lambda b, i: (0,) * len(shape))
    return pl.pallas_call(
        functools.partial(_mixer_kernel, alpha=alpha, ts=ts, aw=aw),
        grid=(bsz, seq // ts),
        in_specs=[
            pl.BlockSpec((1, ts, d), lambda b, i: (b, i, 0)),
            const(d, w_in.shape[1]), const(1, w_in.shape[1]), const(1, aw), const(1, aw),
            const(A_HEADS, CHUNK, CHUNK), const(CHUNK, aw), const(conv_w.shape[0], bw),
            const(aw + bw, d), const(1, d), const(1, d),
        ],
        out_specs=pl.BlockSpec((1, ts, d), lambda b, i: (b, i, 0)),
        out_shape=jax.ShapeDtypeStruct(x.shape, F32),
        scratch_shapes=[pltpu.VMEM((ts + 2 * SUBLANES, bw), F32)],
        compiler_params=pltpu.CompilerParams(
            dimension_semantics=("parallel", "arbitrary"), vmem_limit_bytes=VMEM_LIMIT),
        name="mixer",
    )(x, w_in.astype(BF16), b_in.reshape(1, -1), sgu_g.reshape(1, -1), sgu_b.reshape(1, -1),
      ws, bsp, conv_w.reshape(conv_w.shape[0], bw), w_out.astype(BF16),
      ln_g.reshape(1, -1), ln_b.reshape(1, -1))


def _kv_kernel(m_ref, w_ref, o_ref):
    o_ref[0] = _dot(m_ref[0].astype(BF16), w_ref[...]).astype(BF16)


def _kv_proj(mem, w_kv):
    bsz, m, d = mem.shape
    n = w_kv.shape[1]
    return pl.pallas_call(
        _kv_kernel,
        grid=(bsz,),
        in_specs=[pl.BlockSpec((1, m, d), lambda b: (b, 0, 0)),
                  pl.BlockSpec((d, n), lambda b: (0, 0))],
        out_specs=pl.BlockSpec((1, m, n), lambda b: (b, 0, 0)),
        out_shape=jax.ShapeDtypeStruct((bsz, m, n), BF16),
        compiler_params=pltpu.CompilerParams(
            dimension_semantics=("parallel",), vmem_limit_bytes=VMEM_LIMIT),
        name="kv_proj",
    )(mem, w_kv.astype(BF16))


def _xattn_kernel(x_ref, kv_ref, wq_ref, wo_ref, lg_ref, lb_ref, o_ref, *, alpha):
    x = x_ref[0]
    d = x.shape[-1]
    dk = d // X_HEADS
    q = _dot(x.astype(BF16), wq_ref[...])
    outs = []
    for h in range(X_HEADS):
        qh = q[:, h * dk:(h + 1) * dk].astype(BF16)
        s = _dot_nt(qh, kv_ref[0, :, h * dk:(h + 1) * dk]) * np.float32(dk ** -0.5)
        e = jnp.exp(s - jnp.max(s, axis=-1, keepdims=True))
        p = e / jnp.sum(e, axis=-1, keepdims=True)
        outs.append(_dot(p.astype(BF16), kv_ref[0, :, d + h * dk:d + (h + 1) * dk]))
    o = jnp.concatenate(outs, axis=-1).astype(BF16)
    o_ref[0] = _ln(alpha * x + _dot(o, wo_ref[...]), lg_ref[...], lb_ref[...])


def _xattn(x, kv, w_q, w_o, ln_g, ln_b, alpha, ts):
    bsz, seq, d = x.shape
    m = kv.shape[1]
    return pl.pallas_call(
        functools.partial(_xattn_kernel, alpha=alpha),
        grid=(bsz, seq // ts),
        in_specs=[
            pl.BlockSpec((1, ts, d), lambda b, i: (b, i, 0)),
            pl.BlockSpec((1, m, 2 * d), lambda b, i: (b, 0, 0)),
            pl.BlockSpec((d, d), lambda b, i: (0, 0)),
            pl.BlockSpec((d, d), lambda b, i: (0, 0)),
            pl.BlockSpec((1, d), lambda b, i: (0, 0)),
            pl.BlockSpec((1, d), lambda b, i: (0, 0)),
        ],
        out_specs=pl.BlockSpec((1, ts, d), lambda b, i: (b, i, 0)),
        out_shape=jax.ShapeDtypeStruct(x.shape, F32),
        compiler_params=pltpu.CompilerParams(
            dimension_semantics=("parallel", "parallel"), vmem_limit_bytes=VMEM_LIMIT),
        name="xattn",
    )(x, kv, w_q.astype(BF16), w_o.astype(BF16), ln_g.reshape(1, -1), ln_b.reshape(1, -1))


def _top_pairs(va, vb, ea, eb, rounds):
    assert rounds == 2 * SUBLANES
    lanes = va.shape[1]
    sub = lax.broadcasted_iota(jnp.int32, (SUBLANES, lanes), 0).astype(F32)
    big = np.float32(2 ** 24)
    v, e = [], []
    for i in range(rounds):
        col = va[i:i + 1] + vb[0:SUBLANES]
        live = rounds // (i + 1)
        v.append(col if live >= SUBLANES else jnp.where(sub < np.float32(live), col, -jnp.inf))
        e.append(ea[i:i + 1] + eb[0:SUBLANES])
    v_hi = va[0:1] + vb[SUBLANES:]
    e_hi = ea[0:1] + eb[SUBLANES:]
    code = sub
    code_hi = sub + np.float32(SUBLANES)
    tops, picks = [], []
    for r in range(rounds):
        m = jnp.max(jnp.maximum(v[0], v_hi), axis=0, keepdims=True)
        first = jnp.min(jnp.minimum(jnp.where(v[0] == m, code, big), jnp.where(v_hi == m, code_hi, big)),
                        axis=0, keepdims=True)
        hit = code == first
        hit_hi = code_hi == first
        tops.append(m)
        picks.append(jnp.max(jnp.maximum(jnp.where(hit, e[0], -1.0), jnp.where(hit_hi, e_hi, -1.0)),
                             axis=0, keepdims=True))
        v_hi = jnp.where(hit_hi, -jnp.inf, v_hi)
        code = jnp.where(hit, code + np.float32(rounds), code)
        for i in range(rounds - 1 - r):
            v[i] = jnp.where(hit, v[i + 1], v[i])
            e[i] = jnp.where(hit, e[i + 1], e[i])
    return jnp.concatenate(tops, axis=0), jnp.concatenate(picks, axis=0)


def _top_rows(vals, rounds):
    n, lanes = vals.shape
    nt = n // SUBLANES
    assert rounds <= nt
    sub = lax.broadcasted_iota(jnp.int32, (SUBLANES, lanes), 0).astype(F32)
    v = [vals[k * SUBLANES:(k + 1) * SUBLANES] for k in range(nt)]
    idx = [sub + np.float32(SUBLANES * k) for k in range(nt)]
    for p in range(nt):
        for k in range(p % 2, nt - 1, 2):
            gt = v[k + 1] > v[k]
            v[k], v[k + 1] = jnp.where(gt, v[k + 1], v[k]), jnp.where(gt, v[k], v[k + 1])
            idx[k], idx[k + 1] = jnp.where(gt, idx[k + 1], idx[k]), jnp.where(gt, idx[k], idx[k + 1])
    tops, picks = [], []
    for r in range(rounds):
        m = jnp.max(v[0], axis=0, keepdims=True)
        first = jnp.min(jnp.where(v[0] == m, idx[0], np.float32(2 ** 24)), axis=0, keepdims=True)
        tops.append(m)
        picks.append(first)
        hit = idx[0] == first
        for k in range(rounds - 1 - r):
            v[k] = jnp.where(hit, v[k + 1], v[k])
            idx[k] = jnp.where(hit, idx[k + 1], idx[k])
    return jnp.concatenate(tops, axis=0), jnp.concatenate(picks, axis=0)


def _route_kernel(x_ref, wq_ref, keys_ref, idx_ref, gate_ref, idx_t, gate_t, sc_ref, top_ref):
    xb = x_ref[...].astype(BF16)
    q =_dot(xb, wq_ref[...]).astype(BF16)
    hk = q.shape[1] // (2 * PEER_HEADS)
    for i in range(2 * PEER_HEADS):
        sc_ref[i] = _dot_nt(keys_ref[i // 2, i % 2], q[:, i * hk:(i + 1) * hk])
    top_ref[1] = jnp.zeros(top_ref.shape[1:], F32)

    def sub_top(h, slot):
        va, ia = _top_rows(sc_ref[2 * h], TOPK)
        vb, ib = _top_rows(sc_ref[2 * h + 1], TOPK)
        top_ref[slot] = jnp.concatenate([va, vb, ia, ib], axis=0)

    def pair_top(h, slot):
        t = top_ref[slot]
        va, vb, ia, ib = (t[i * TOPK:(i + 1) * TOPK] for i in range(4))
        top, expert = _top_pairs(va, vb, ia * np.float32(N_KEYS), ib, TOPK)
        e = jnp.exp(top - top[0:1])
        r0 = pl.multiple_of(h * TOPK, TOPK)
        gate_t[pl.ds(r0, TOPK), :] = e / jnp.sum(e, axis=0, keepdims=True)
        idx_t[pl.ds(r0, TOPK), :] = expert.astype(jnp.int32) * ROW_WORDS

    def two_heads(p, carry):
        h = 2 * p
        sub_top(h, 0)
        pair_top(jnp.maximum(h - 1, 0), 1)
        sub_top(h + 1, 1)
        pair_top(h, 0)
        return carry

    lax.fori_loop(0, PEER_HEADS // 2, two_heads, 0)
    pair_top(PEER_HEADS - 1, 1)
    idx_ref[...] = idx_t[...].T
    gate_ref[...] = gate_t[...].T


def _peer_route(xt, w_query, sub_keys, tt):
    n_tok, d = xt.shape
    wq = w_query.astype(BF16)
    hk = PEER_HEADS * TOPK
    return pl.pallas_call(
        _route_kernel,
        grid=(n_tok // tt,),
        in_specs=[
            pl.BlockSpec((tt, d), lambda i: (i, 0)),
            pl.BlockSpec(wq.shape, lambda i: (0, 0)),
            pl.BlockSpec(sub_keys.shape, lambda i: (0, 0, 0, 0)),
        ],
        out_specs=[pl.BlockSpec((tt, hk), lambda i: (i, 0)),
                   pl.BlockSpec((tt, hk), lambda i: (i, 0))],
        out_shape=[jax.ShapeDtypeStruct((n_tok, hk), jnp.int32),
                   jax.ShapeDtypeStruct((n_tok, hk), F32)],
        scratch_shapes=[pltpu.VMEM((hk, tt), jnp.int32), pltpu.VMEM((hk, tt), F32),
                        pltpu.VMEM((2 * PEER_HEADS, N_KEYS, tt), F32), pltpu.VMEM((2, 4 * TOPK, tt), F32)],
        compiler_params=pltpu.CompilerParams(
            dimension_semantics=("parallel",), vmem_limit_bytes=VMEM_LIMIT),
        name="peer_route",
    )(xt, wq, sub_keys.astype(BF16))


TOK_GROUP = 64


def _pack_kernel(w_ref, o_ref):
    n = w_ref.shape[0]
    for s in range(ROW_WORDS):
        lo = w_ref[:, 2 * s * LANES:(2 * s + 1) * LANES].astype(BF16).astype(F32)
        hi = w_ref[:, (2 * s + 1) * LANES:(2 * s + 2) * LANES].astype(BF16).astype(F32)
        word = pltpu.bitcast(hi, jnp.uint32) | (pltpu.bitcast(lo, jnp.uint32) >> 16)
        o_ref[pl.ds(s, n, stride=ROW_WORDS), :] = word


def _pack_rows(w, rows):
    n, d = w.shape
    assert d == 2 * ROW_WORDS * LANES and n % rows == 0
    return pl.pallas_call(
        _pack_kernel,
        grid=(n // rows,),
        in_specs=[pl.BlockSpec((rows, d), lambda i: (i, 0))],
        out_specs=pl.BlockSpec((rows * ROW_WORDS, LANES), lambda i: (i, 0)),
        out_shape=jax.ShapeDtypeStruct((n * ROW_WORDS, LANES), jnp.uint32),
        compiler_params=pltpu.CompilerParams(
            dimension_semantics=("parallel",), vmem_limit_bytes=VMEM_LIMIT),
        name="pack_rows",
    )(w)


def _gather_rows(idx_ref, t, tab_ref, hk):
    words = [tab_ref[pl.ds(pl.multiple_of(idx_ref[t, k], ROW_WORDS), ROW_WORDS), :] for k in range(hk)]
    return pltpu.bitcast(jnp.concatenate(words, axis=0), BF16)


def _diag_mask(hk):
    r = lax.broadcasted_iota(jnp.int32, (SUBLANES, hk * SUBLANES), 0)
    c = lax.broadcasted_iota(jnp.int32, (SUBLANES, hk * SUBLANES), 1)
    return (c & (SUBLANES - 1)) == r


def _idx_copy(idx_hbm, row0, buf, sem):
    return pltpu.make_async_copy(idx_hbm.at[pl.ds(row0, TOK_GROUP)], buf, sem)


def _for_each_group(idx_hbm, tb, idx_a, idx_b, sems, process):
    step = pl.program_id(0)
    base = step * tb
    last = idx_hbm.shape[0] - TOK_GROUP

    @pl.when(step == 0)
    def _():
        _idx_copy(idx_hbm, 0, idx_a, sems.at[0]).start()

    def pair(i, carry):
        t0 = pl.multiple_of(i * (2 * TOK_GROUP), 2 * TOK_GROUP)
        _idx_copy(idx_hbm, base + t0 + TOK_GROUP, idx_b, sems.at[1]).start()
        _idx_copy(idx_hbm, 0, idx_a, sems.at[0]).wait()
        process(idx_a, t0, 0)
        _idx_copy(idx_hbm, jnp.minimum(base + t0 + 2 * TOK_GROUP, last), idx_a, sems.at[0]).start()
        _idx_copy(idx_hbm, 0, idx_b, sems.at[1]).wait()
        process(idx_b, t0 + TOK_GROUP, 1)
        return carry

    lax.fori_loop(0, tb // (2 * TOK_GROUP), pair, 0)

    @pl.when(step == pl.num_programs(0) - 1)
    def _():
        _idx_copy(idx_hbm, 0, idx_a, sems.at[0]).wait()


def _idx_scratch(hk):
    return [pltpu.SMEM((TOK_GROUP, hk), jnp.int32), pltpu.SMEM((TOK_GROUP, hk), jnp.int32),
            pltpu.SemaphoreType.DMA((2,))]


def _down_kernel(idx_hbm, x_ref, gate_ref, fold_ref, tab_ref, c_ref, idx_a, idx_b, sems, xt_ref,
                 part_a, part_b):
    hk = gate_ref.shape[1]
    tb = x_ref.shape[0]
    mask = _diag_mask(hk)
    parts = (part_a, part_b)

    def finish(part_ref, t0):
        z = _dot(part_ref[...].astype(BF16), fold_ref[...])
        act = jnp.concatenate(
            [jnp.sum(z[j * SUBLANES:(j + 1) * SUBLANES], axis=0, keepdims=True)
             for j in range(TOK_GROUP)], axis=0)
        c_ref[pl.ds(t0, TOK_GROUP), :] = gate_ref[pl.ds(t0, TOK_GROUP), :] * _gelu(act)

    def process(idx_ref, t0, slot):
        finish(parts[1 - slot], pl.multiple_of(jnp.maximum(t0 - TOK_GROUP, 0), TOK_GROUP))
        xs = x_ref[pl.ds(t0, TOK_GROUP), :]
        for r in range(SUBLANES):
            xt_ref[pl.ds(r, TOK_GROUP, stride=SUBLANES), :] = xs[:, r * LANES:(r + 1) * LANES]
        for j in range(TOK_GROUP):
            rows = _gather_rows(idx_ref, j, tab_ref, hk)
            xj = xt_ref[j * SUBLANES:(j + 1) * SUBLANES, :].astype(BF16)
            s = _dot_nt(xj, rows)
            parts[slot][j * SUBLANES:(j + 1) * SUBLANES, :] = jnp.where(mask, s, 0.0)

    part_b[...] = jnp.zeros(part_b.shape, F32)
    _for_each_group(idx_hbm, tb, idx_a, idx_b, sems, process)
    finish(part_b, tb - TOK_GROUP)


def _peer_down(idx, xt, gate, tab, tb):
    n_tok, hk = idx.shape
    assert tb % (2 * TOK_GROUP) == 0 and xt.shape[1] == SUBLANES * LANES
    fold = (jnp.arange(hk * SUBLANES)[:, None] // SUBLANES == jnp.arange(hk)[None, :]).astype(BF16)
    return pl.pallas_call(
        _down_kernel,
        grid=(n_tok // tb,),
        in_specs=[
            pl.BlockSpec(memory_space=pl.ANY),
            pl.BlockSpec((tb, SUBLANES * LANES), lambda i: (i, 0)),
            pl.BlockSpec((tb, hk), lambda i: (i, 0)),
            pl.BlockSpec(fold.shape, lambda i: (0, 0)),
            pl.BlockSpec(memory_space=pltpu.VMEM),
        ],
        out_specs=pl.BlockSpec((tb, hk), lambda i: (i, 0)),
        out_shape=jax.ShapeDtypeStruct((n_tok, hk), F32),
        scratch_shapes=_idx_scratch(hk) + [pltpu.VMEM((TOK_GROUP * SUBLANES, LANES), F32)]
        + [pltpu.VMEM((TOK_GROUP * SUBLANES, hk * SUBLANES), F32) for _ in range(2)],
        compiler_params=pltpu.CompilerParams(
            dimension_semantics=("arbitrary",), vmem_limit_bytes=VMEM_LIMIT),
        name="peer_down",
    )(idx, xt, gate, fold, tab)


def _up_kernel(idx_hbm, c_ref, x_ref, spread_ref, tab_ref, g_ref, b_ref, o_ref, idx_a, idx_b, sems,
               ht_a, ht_b, *, alpha):
    hk = c_ref.shape[1]
    tb = c_ref.shape[0]
    mask = _diag_mask(hk)
    hts = (ht_a, ht_b)

    def finish(ht_ref, t0):
        h = jnp.concatenate([ht_ref[pl.ds(r, TOK_GROUP, stride=SUBLANES), :] for r in range(SUBLANES)],
                            axis=-1)
        o_ref[pl.ds(t0, TOK_GROUP), :] = _ln(alpha * x_ref[pl.ds(t0, TOK_GROUP), :] + h,
                                              g_ref[...], b_ref[...])

    def process(idx_ref, t0, slot):
        finish(hts[1 - slot], pl.multiple_of(jnp.maximum(t0 - TOK_GROUP, 0), TOK_GROUP))
        cexp = _dot(c_ref[pl.ds(t0, TOK_GROUP), :].astype(BF16), spread_ref[...])
        for j in range(TOK_GROUP):
            rows = _gather_rows(idx_ref, j, tab_ref, hk)
            cj = jnp.where(mask, jnp.broadcast_to(cexp[j:j + 1, :], mask.shape), 0.0)
            hts[slot][j * SUBLANES:(j + 1) * SUBLANES, :] = _dot(cj.astype(BF16), rows)

    ht_b[...] = jnp.zeros(ht_b.shape, F32)
    _for_each_group(idx_hbm, tb, idx_a, idx_b, sems, process)
    finish(ht_b, tb - TOK_GROUP)


def _peer_up(idx, c, xt, tab, ln_g, ln_b, alpha, tb):
    n_tok, hk = idx.shape
    d = xt.shape[1]
    assert tb % (2 * TOK_GROUP) == 0 and d == SUBLANES * LANES
    spread = (jnp.arange(hk)[:, None] == jnp.arange(hk * SUBLANES)[None, :] // SUBLANES).astype(BF16)
    return pl.pallas_call(
        functools.partial(_up_kernel, alpha=alpha),
        grid=(n_tok // tb,),
        in_specs=[
            pl.BlockSpec(memory_space=pl.ANY),
            pl.BlockSpec((tb, hk), lambda i: (i, 0)),
            pl.BlockSpec((tb, d), lambda i: (i, 0)),
            pl.BlockSpec(spread.shape, lambda i: (0, 0)),
            pl.BlockSpec(memory_space=pltpu.VMEM),
            pl.BlockSpec((1, d), lambda i: (0, 0)),
            pl.BlockSpec((1, d), lambda i: (0, 0)),
        ],
        out_specs=pl.BlockSpec((tb, d), lambda i: (i, 0)),
        out_shape=jax.ShapeDtypeStruct((n_tok, d), F32),
        scratch_shapes=_idx_scratch(hk)
        + [pltpu.VMEM((TOK_GROUP * SUBLANES, LANES), F32) for _ in range(2)],
        compiler_params=pltpu.CompilerParams(
            dimension_semantics=("arbitrary",), vmem_limit_bytes=VMEM_LIMIT),
        name="peer_up",
    )(idx, c, xt, spread, tab, ln_g.reshape(1, -1), ln_b.reshape(1, -1))


def _tile(n, want):
    return want if n % want == 0 else n


def kernel(x, mem, w_in, b_in, sgu_g, sgu_b, w_spatial, b_spatial, conv_w, w_mix_out, ln1_g, ln1_b,
           w_xq, w_xkv, w_xo, ln2_g, ln2_b, w_pq, sub_keys, w_down, w_up, ln3_g, ln3_b):
    depth = w_in.shape[0]
    bsz, seq, d = x.shape
    assert d == SUBLANES * LANES and seq % CHUNK == 0
    alpha = np.float32((2 * depth) ** 0.25)
    n_tok = bsz * seq
    ts = _tile(seq, 512)
    for l in range(depth):
        x = _mixer(x, w_in[l], b_in[l], sgu_g[l], sgu_b[l], w_spatial[l], b_spatial[l], conv_w[l],
                   w_mix_out[l], ln1_g[l], ln1_b[l], alpha, ts)
        kv = _kv_proj(mem, w_xkv[l])
        x = _xattn(x, kv, w_xq[l], w_xo[l], ln2_g[l], ln2_b[l], alpha, ts)
        xt = x.reshape(n_tok, d)
        idx, gate = _peer_route(xt, w_pq[l], sub_keys[l], _tile(n_tok, 512))
        tb = _tile(n_tok, 512)
        pack = _tile(w_down.shape[1], 512)
        c = _peer_down(idx, xt, gate, _pack_rows(w_down[l], pack), tb)
        x = _peer_up(idx, c, xt, _pack_rows(w_up[l], pack), ln3_g[l], ln3_b[l], alpha, tb)
        x = x.reshape(bsz, seq, d)
    return x
```

```python
import functools

import jax
import jax.numpy as jnp
import numpy as np
from jax import lax
from jax.experimental import pallas as pl
from jax.experimental.pallas import tpu as pltpu

LN_EPS = 1e-5
LANES = 128
SUBLANES = 8
A_HEADS = 4
CHUNK = 128
X_HEADS = 4
PEER_HEADS = 8
N_KEYS = 128
TOPK = 16
ROW_WORDS = 4
VMEM_LIMIT = 56 * 1024 * 1024

F32 = jnp.float32
BF16 = jnp.bfloat16


def _gelu(x):
    return 0.5 * x * (1.0 + lax.erf(x * np.float32(2.0 ** -0.5)))


def _ln(v, g, b):
    mu = jnp.mean(v, axis=-1, keepdims=True)
    d = v - mu
    var = jnp.mean(d * d, axis=-1, keepdims=True)
    return d * lax.rsqrt(var + LN_EPS) * g + b


def _dot(a, b):
    return jnp.dot(a, b, preferred_element_type=F32)


def _dot_nt(a, b):
    return lax.dot_general(a, b, (((1,), (1,)), ((), ())), preferred_element_type=F32)


def _mixer_kernel(x_ref, win_ref, bin_ref, sg_ref, sb_ref, ws_ref, bsp_ref, cw_ref, wout_ref,
                  lg_ref, lb_ref, o_ref, zc_ref, *, alpha, ts, aw):
    @pl.when(pl.program_id(1) == 0)
    def _():
        zc_ref[0:SUBLANES, :] = jnp.zeros((SUBLANES, zc_ref.shape[1]), F32)

    x = x_ref[0]
    xb = x.astype(BF16)

    def zcol(lo, hi):
        return _dot(xb, win_ref[:, lo:hi]) + bin_ref[:, lo:hi]

    hd = aw // A_HEADS
    u = _gelu(zcol(0, aw))
    v = _gelu(zcol(aw, 2 * aw))
    mixed = []
    for h in range(A_HEADS):
        sl = slice(h * hd, (h + 1) * hd)
        vh = _ln(v[:, sl], sg_ref[:, sl], sb_ref[:, sl]).astype(BF16)
        mixed.append(jnp.concatenate(
            [_dot(ws_ref[h], vh[c * CHUNK:(c + 1) * CHUNK]) + bsp_ref[:, sl]
             for c in range(ts // CHUNK)], axis=0))
    ya = u * jnp.concatenate(mixed, axis=-1)

    bw = zc_ref.shape[1]
    hb = zcol(2 * aw, 2 * aw + bw)
    gb = zcol(2 * aw + bw, 2 * aw + 2 * bw)
    gc = zcol(2 * aw + 2 * bw, 2 * aw + 3 * bw)
    zc_ref[SUBLANES:SUBLANES + ts, :] = gc * hb
    conv = (cw_ref[0:1, :] * zc_ref[SUBLANES - 2:SUBLANES - 2 + ts, :]
            + cw_ref[1:2, :] * zc_ref[SUBLANES - 1:SUBLANES - 1 + ts, :]
            + cw_ref[2:3, :] * zc_ref[SUBLANES:SUBLANES + ts, :])
    yb = gb * conv
    zc_ref[0:SUBLANES, :] = zc_ref[ts:ts + SUBLANES, :]

    h = _dot(ya.astype(BF16), wout_ref[0:aw, :]) + _dot(yb.astype(BF16), wout_ref[aw:, :])
    o_ref[0] = _ln(alpha * x + h, lg_ref[...], lb_ref[...])


def _mixer(x, w_in, b_in, sgu_g, sgu_b, w_spatial, b_spatial, conv_w, w_out, ln_g, ln_b, alpha, ts):
    bsz, seq, d = x.shape
    aw = A_HEADS * sgu_g.shape[-1]
    bw = conv_w.shape[-1]
    causal = jnp.tril(jnp.ones((CHUNK, CHUNK), dtype=bool))
    ws = jnp.where(causal[None], w_spatial, 0.0).astype(BF16)
    bsp = jnp.repeat(b_spatial.T, aw // A_HEADS, axis=1)
    const = lambda *shape: pl.BlockSpec(shape, lambda b, i: (0,) * len(shape))
    return pl.pallas_call(
        functools.partial(_mixer_kernel, alpha=alpha, ts=ts, aw=aw),
        grid=(bsz, seq // ts),
        in_specs=[
            pl.BlockSpec((1, ts, d), lambda b, i: (b, i, 0)),
            const(d, w_in.shape[1]), const(1, w_in.shape[1]), const(1, aw), const(1, aw),
            const(A_HEADS, CHUNK, CHUNK), const(CHUNK, aw), const(conv_w.shape[0], bw),
            const(aw + bw, d), const(1, d), const(1, d),
        ],
        out_specs=pl.BlockSpec((1, ts, d), lambda b, i: (b, i, 0)),
        out_shape=jax.ShapeDtypeStruct(x.shape, F32),
        scratch_shapes=[pltpu.VMEM((ts + 2 * SUBLANES, bw), F32)],
        compiler_params=pltpu.CompilerParams(
            dimension_semantics=("parallel", "arbitrary"), vmem_limit_bytes=VMEM_LIMIT),
        name="mixer",
    )(x, w_in.astype(BF16), b_in.reshape(1, -1), sgu_g.reshape(1, -1), sgu_b.reshape(1, -1),
      ws, bsp, conv_w.reshape(conv_w.shape[0], bw), w_out.astype(BF16),
      ln_g.reshape(1, -1), ln_b.reshape(1, -1))


def _kv_kernel(m_ref, w_ref, o_ref):
    o_ref[0] = _dot(m_ref[0].astype(BF16), w_ref[...]).astype(BF16)


def _kv_proj(mem, w_kv):
    bsz, m, d = mem.shape
    n = w_kv.shape[1]
    return pl.pallas_call(
        _kv_kernel,
        grid=(bsz,),
        in_specs=[pl.BlockSpec((1, m, d), lambda b: (b, 0, 0)),
                  pl.BlockSpec((d, n), lambda b: (0, 0))],
        out_specs=pl.BlockSpec((1, m, n), lambda b: (b, 0, 0)),
        out_shape=jax.ShapeDtypeStruct((bsz, m, n), BF16),
        compiler_params=pltpu.CompilerParams(
            dimension_semantics=("parallel",), vmem_limit_bytes=VMEM_LIMIT),
        name="kv_proj",
    )(mem, w_kv.astype(BF16))


def _xattn_kernel(x_ref, kv_ref, wq_ref, wo_ref, lg_ref, lb_ref, o_ref, *, alpha):
    x = x_ref[0]
    d = x.shape[-1]
    dk = d // X_HEADS
    q = _dot(x.astype(BF16), wq_ref[...])
    outs = []
    for h in range(X_HEADS):
        qh = q[:, h * dk:(h + 1) * dk].astype(BF16)
        s = _dot_nt(qh, kv_ref[0, :, h * dk:(h + 1) * dk]) * np.float32(dk ** -0.5)
        e = jnp.exp(s - jnp.max(s, axis=-1, keepdims=True))
        p = e / jnp.sum(e, axis=-1, keepdims=True)
        outs.append(_dot(p.astype(BF16), kv_ref[0, :, d + h * dk:d + (h + 1) * dk]))
    o = jnp.concatenate(outs, axis=-1).astype(BF16)
    o_ref[0] = _ln(alpha * x + _dot(o, wo_ref[...]), lg_ref[...], lb_ref[...])


def _xattn(x, kv, w_q, w_o, ln_g, ln_b, alpha, ts):
    bsz, seq, d = x.shape
    m = kv.shape[1]
    return pl.pallas_call(
        functools.partial(_xattn_kernel, alpha=alpha),
        grid=(bsz, seq // ts),
        in_specs=[
            pl.BlockSpec((1, ts, d), lambda b, i: (b, i, 0)),
            pl.BlockSpec((1, m, 2 * d), lambda b, i: (b, 0, 0)),
            pl.BlockSpec((d, d), lambda b, i: (0, 0)),
            pl.BlockSpec((d, d), lambda b, i: (0, 0)),
            pl.BlockSpec((1, d), lambda b, i: (0, 0)),
            pl.BlockSpec((1, d), lambda b, i: (0, 0)),
        ],
        out_specs=pl.BlockSpec((1, ts, d), lambda b, i: (b, i, 0)),
        out_shape=jax.ShapeDtypeStruct(x.shape, F32),
        compiler_params=pltpu.CompilerParams(
            dimension_semantics=("parallel", "parallel"), vmem_limit_bytes=VMEM_LIMIT),
        name="xattn",
    )(x, kv, w_q.astype(BF16), w_o.astype(BF16), ln_g.reshape(1, -1), ln_b.reshape(1, -1))


def _top_pairs(va, vb, ea, eb, rounds):
    assert rounds == 2 * SUBLANES
    lanes = va.shape[1]
    sub = lax.broadcasted_iota(jnp.int32, (SUBLANES, lanes), 0).astype(F32)
    big = np.float32(2 ** 24)
    v, e = [], []
    for i in range(rounds):
        col = va[i:i + 1] + vb[0:SUBLANES]
        live = rounds // (i + 1)
        v.append(col if live >= SUBLANES else jnp.where(sub < np.float32(live), col, -jnp.inf))
        e.append(ea[i:i + 1] + eb[0:SUBLANES])
    v_hi = va[0:1] + vb[SUBLANES:]
    e_hi = ea[0:1] + eb[SUBLANES:]
    code = sub
    code_hi = sub + np.float32(SUBLANES)
    tops, picks = [], []
    for r in range(rounds):
        m = jnp.max(jnp.maximum(v[0], v_hi), axis=0, keepdims=True)
        first = jnp.min(jnp.minimum(jnp.where(v[0] == m, code, big), jnp.where(v_hi == m, code_hi, big)),
                        axis=0, keepdims=True)
        hit = code == first
        hit_hi = code_hi == first
        tops.append(m)
        picks.append(jnp.max(jnp.maximum(jnp.where(hit, e[0], -1.0), jnp.where(hit_hi, e_hi, -1.0)),
                             axis=0, keepdims=True))
        v_hi = jnp.where(hit_hi, -jnp.inf, v_hi)
        code = jnp.where(hit, code + np.float32(rounds), code)
        for i in range(rounds - 1 - r):
            v[i] = jnp.where(hit, v[i + 1], v[i])
            e[i] = jnp.where(hit, e[i + 1], e[i])
    return jnp.concatenate(tops, axis=0), jnp.concatenate(picks, axis=0)


def _top_rows(vals, rounds):
    n, lanes = vals.shape
    nt = n // SUBLANES
    assert rounds <= nt
    sub = lax.broadcasted_iota(jnp.int32, (SUBLANES, lanes), 0).astype(F32)
    v = [vals[k * SUBLANES:(k + 1) * SUBLANES] for k in range(nt)]
    idx = [sub + np.float32(SUBLANES * k) for k in range(nt)]
    for p in range(nt):
        for k in range(p % 2, nt - 1, 2):
            gt = v[k + 1] > v[k]
            v[k], v[k + 1] = jnp.where(gt, v[k + 1], v[k]), jnp.where(gt, v[k], v[k + 1])
            idx[k], idx[k + 1] = jnp.where(gt, idx[k + 1], idx[k]), jnp.where(gt, idx[k], idx[k + 1])
    tops, picks = [], []
    for r in range(rounds):
        m = jnp.max(v[0], axis=0, keepdims=True)
        first = jnp.min(jnp.where(v[0] == m, idx[0], np.float32(2 ** 24)), axis=0, keepdims=True)
        tops.append(m)
        picks.append(first)
        hit = idx[0] == first
        for k in range(rounds - 1 - r):
            v[k] = jnp.where(hit, v[k + 1], v[k])
            idx[k] = jnp.where(hit, idx[k + 1], idx[k])
    return jnp.concatenate(tops, axis=0), jnp.concatenate(picks, axis=0)


def _route_kernel(x_ref, wq_ref, keys_ref, idx_ref, gate_ref, idx_t, gate_t, sc_ref, top_ref):
    xb = x_ref[...].astype(BF16)
    q = _dot(xb, wq_ref[...]).astype(BF16)
    hk = q.shape[1] // (2 * PEER_HEADS)
    for i in range(2 * PEER_HEADS):
        sc_ref[i] = _dot_nt(keys_ref[i // 2, i % 2], q[:, i * hk:(i + 1) * hk])
    top_ref[1] = jnp.zeros(top_ref.shape[1:], F32)

    def sub_top(h, slot):
        va, ia = _top_rows(sc_ref[2 * h], TOPK)
        vb, ib = _top_rows(sc_ref[2 * h + 1], TOPK)
        top_ref[slot] = jnp.concatenate([va, vb, ia, ib], axis=0)

    def pair_top(h, slot):
        t = top_ref[slot]
        va, vb, ia, ib = (t[i * TOPK:(i + 1) * TOPK] for i in range(4))
        top, expert = _top_pairs(va, vb, ia * np.float32(N_KEYS), ib, TOPK)
        e = jnp.exp(top - top[0:1])
        r0 = pl.multiple_of(h * TOPK, TOPK)
        gate_t[pl.ds(r0, TOPK), :] = e / jnp.sum(e, axis=0, keepdims=True)
        idx_t[pl.ds(r0, TOPK), :] = expert.astype(jnp.int32) * ROW_WORDS

    def two_heads(p, carry):
        h = 2 * p
        sub_top(h, 0)
        pair_top(jnp.maximum(h - 1, 0), 1)
        sub_top(h + 1, 1)
        pair_top(h, 0)
        return carry

    lax.fori_loop(0, PEER_HEADS // 2, two_heads, 0)
    pair_top(PEER_HEADS - 1, 1)
    idx_ref[...] = idx_t[...].T
    gate_ref[...] = gate_t[...].T


def _peer_route(xt, w_query, sub_keys, tt):
    n_tok, d = xt.shape
    wq = w_query.astype(BF16)
    hk = PEER_HEADS * TOPK
    return pl.pallas_call(
        _route_kernel,
        grid=(n_tok // tt,),
        in_specs=[
            pl.BlockSpec((tt, d), lambda i: (i, 0)),
            pl.BlockSpec(wq.shape, lambda i: (0, 0)),
            pl.BlockSpec(sub_keys.shape, lambda i: (0, 0, 0, 0)),
        ],
        out_specs=[pl.BlockSpec((tt, hk), lambda i: (i, 0)),
                   pl.BlockSpec((tt, hk), lambda i: (i, 0))],
        out_shape=[jax.ShapeDtypeStruct((n_tok, hk), jnp.int32),
                   jax.ShapeDtypeStruct((n_tok, hk), F32)],
        scratch_shapes=[pltpu.VMEM((hk, tt), jnp.int32), pltpu.VMEM((hk, tt), F32),
                        pltpu.VMEM((2 * PEER_HEADS, N_KEYS, tt), F32), pltpu.VMEM((2, 4 * TOPK, tt), F32)],
        compiler_params=pltpu.CompilerParams(
            dimension_semantics=("parallel",), vmem_limit_bytes=VMEM_LIMIT),
        name="peer_route",
    )(xt, wq, sub_keys.astype(BF16))


TOK_GROUP = 128


def _pack_kernel(w_ref, o_ref):
    n = w_ref.shape[0]
    for s in range(ROW_WORDS):
        lo = w_ref[:, 2 * s * LANES:(2 * s + 1) * LANES].astype(BF16).astype(F32)
        hi = w_ref[:, (2 * s + 1) * LANES:(2 * s + 2) * LANES].astype(BF16).astype(F32)
        word = pltpu.bitcast(hi, jnp.uint32) | (pltpu.bitcast(lo, jnp.uint32) >> 16)
        o_ref[pl.ds(s, n, stride=ROW_WORDS), :] = word


def _pack_rows(w, rows):
    n, d = w.shape
    assert d == 2 * ROW_WORDS * LANES and n % rows == 0
    return pl.pallas_call(
        _pack_kernel,
        grid=(n // rows,),
        in_specs=[pl.BlockSpec((rows, d), lambda i: (i, 0))],
        out_specs=pl.BlockSpec((rows * ROW_WORDS, LANES), lambda i: (i, 0)),
        out_shape=jax.ShapeDtypeStruct((n * ROW_WORDS, LANES), jnp.uint32),
        compiler_params=pltpu.CompilerParams(
            dimension_semantics=("parallel",), vmem_limit_bytes=VMEM_LIMIT),
        name="pack_rows",
    )(w)


def _gather_rows(idx_ref, t, tab_ref, hk):
    words = [tab_ref[pl.ds(pl.multiple_of(idx_ref[t, k], ROW_WORDS), ROW_WORDS), :] for k in range(hk)]
    return pltpu.bitcast(jnp.concatenate(words, axis=0), BF16)


def _diag_mask(hk):
    r = lax.broadcasted_iota(jnp.int32, (SUBLANES, hk * SUBLANES), 0)
    c = lax.broadcasted_iota(jnp.int32, (SUBLANES, hk * SUBLANES), 1)
    return (c & (SUBLANES - 1)) == r


def _idx_copy(idx_hbm, row0, buf, sem):
    return pltpu.make_async_copy(idx_hbm.at[pl.ds(row0, TOK_GROUP)], buf, sem)


def _for_each_group(idx_hbm, tb, idx_a, idx_b, sems, process):
    step = pl.program_id(0)
    base = step * tb
    last = idx_hbm.shape[0] - TOK_GROUP

    @pl.when(step == 0)
    def _():
        _idx_copy(idx_hbm, 0, idx_a, sems.at[0]).start()

    def pair(i, carry):
        t0 = pl.multiple_of(i * (2 * TOK_GROUP), 2 * TOK_GROUP)
        _idx_copy(idx_hbm, base + t0 + TOK_GROUP, idx_b, sems.at[1]).start()
        _idx_copy(idx_hbm, 0, idx_a, sems.at[0]).wait()
        process(idx_a, t0, 0)
        _idx_copy(idx_hbm, jnp.minimum(base + t0 + 2 * TOK_GROUP, last), idx_a, sems.at[0]).start()
        _idx_copy(idx_hbm, 0, idx_b, sems.at[1]).wait()
        process(idx_b, t0 + TOK_GROUP, 1)
        return carry

    lax.fori_loop(0, tb // (2 * TOK_GROUP), pair, 0)

    @pl.when(step == pl.num_programs(0) - 1)
    def _():
        _idx_copy(idx_hbm, 0, idx_a, sems.at[0]).wait()


def _idx_scratch(hk):
    return [pltpu.SMEM((TOK_GROUP, hk), jnp.int32), pltpu.SMEM((TOK_GROUP, hk), jnp.int32),
            pltpu.SemaphoreType.DMA((2,))]


def _down_kernel(idx_hbm, x_ref, gate_ref, fold_ref, tab_ref, c_ref, idx_a, idx_b, sems, xt_ref,
                 part_a, part_b):
    hk = gate_ref.shape[1]
    tb = x_ref.shape[0]
    mask = _diag_mask(hk)
    parts = (part_a, part_b)

    def finish(part_ref, t0):
        z = _dot(part_ref[...].astype(BF16), fold_ref[...])
        act = jnp.concatenate(
            [jnp.sum(z[j * SUBLANES:(j + 1) * SUBLANES], axis=0, keepdims=True)
             for j in range(TOK_GROUP)], axis=0)
        c_ref[pl.ds(t0, TOK_GROUP), :] = gate_ref[pl.ds(t0, TOK_GROUP), :] * _gelu(act)

    def process(idx_ref, t0, slot):
        finish(parts[1 - slot], pl.multiple_of(jnp.maximum(t0 - TOK_GROUP, 0), TOK_GROUP))
        xs = x_ref[pl.ds(t0, TOK_GROUP), :]
        for r in range(SUBLANES):
            xt_ref[pl.ds(r, TOK_GROUP, stride=SUBLANES), :] = xs[:, r * LANES:(r + 1) * LANES]
        for j in range(TOK_GROUP):
            rows = _gather_rows(idx_ref, j, tab_ref, hk)
            xj = xt_ref[j * SUBLANES:(j + 1) * SUBLANES, :].astype(BF16)
            s = _dot_nt(xj, rows)
            parts[slot][j * SUBLANES:(j + 1) * SUBLANES, :] = jnp.where(mask, s, 0.0)

    part_b[...] = jnp.zeros(part_b.shape, F32)
    _for_each_group(idx_hbm, tb, idx_a, idx_b, sems, process)
    finish(part_b, tb - TOK_GROUP)


def _peer_down(idx, xt, gate, tab, tb):
    n_tok, hk = idx.shape
    assert tb % (2 * TOK_GROUP) == 0 and xt.shape[1] == SUBLANES * LANES
    fold = (jnp.arange(hk * SUBLANES)[:, None] // SUBLANES == jnp.arange(hk)[None, :]).astype(BF16)
    return pl.pallas_call(
        _down_kernel,
        grid=(n_tok // tb,),
        in_specs=[
            pl.BlockSpec(memory_space=pl.ANY),
            pl.BlockSpec((tb, SUBLANES * LANES), lambda i: (i, 0)),
            pl.BlockSpec((tb, hk), lambda i: (i, 0)),
            pl.BlockSpec(fold.shape, lambda i: (0, 0)),
            pl.BlockSpec(memory_space=pltpu.VMEM),
        ],
        out_specs=pl.BlockSpec((tb, hk), lambda i: (i, 0)),
        out_shape=jax.ShapeDtypeStruct((n_tok, hk), F32),
        scratch_shapes=_idx_scratch(hk) + [pltpu.VMEM((TOK_GROUP * SUBLANES, LANES), F32)]
        + [pltpu.VMEM((TOK_GROUP * SUBLANES, hk * SUBLANES), F32) for _ in range(2)],
        compiler_params=pltpu.CompilerParams(
            dimension_semantics=("arbitrary",), vmem_limit_bytes=VMEM_LIMIT),
        name="peer_down",
    )(idx, xt, gate, fold, tab)


def _up_kernel(idx_hbm, c_ref, x_ref, spread_ref, tab_ref, g_ref, b_ref, o_ref, idx_a, idx_b, sems,
               ht_a, ht_b, *, alpha):
    hk = c_ref.shape[1]
    tb = c_ref.shape[0]
    mask = _diag_mask(hk)
    hts = (ht_a, ht_b)

    def finish(ht_ref, t0):
        h = jnp.concatenate([ht_ref[pl.ds(r, TOK_GROUP, stride=SUBLANES), :] for r in range(SUBLANES)],
                            axis=-1)
        o_ref[pl.ds(t0, TOK_GROUP), :] = _ln(alpha * x_ref[pl.ds(t0, TOK_GROUP), :] + h,
                                              g_ref[...], b_ref[...])

    def process(idx_ref, t0, slot):
        finish(hts[1 - slot], pl.multiple_of(jnp.maximum(t0 - TOK_GROUP, 0), TOK_GROUP))
        cexp = _dot(c_ref[pl.ds(t0, TOK_GROUP), :].astype(BF16), spread_ref[...])
        for j in range(TOK_GROUP):
            rows = _gather_rows(idx_ref, j, tab_ref, hk)
            cj = jnp.where(mask, jnp.broadcast_to(cexp[j:j + 1, :], mask.shape), 0.0)
            hts[slot][j * SUBLANES:(j + 1) * SUBLANES, :] = _dot(cj.astype(BF16), rows)

    ht_b[...] = jnp.zeros(ht_b.shape, F32)
    _for_each_group(idx_hbm, tb, idx_a, idx_b, sems, process)
    finish(ht_b, tb - TOK_GROUP)


def _peer_up(idx, c, xt, tab, ln_g, ln_b, alpha, tb):
    n_tok, hk = idx.shape
    d = xt.shape[1]
    assert tb % (2 * TOK_GROUP) == 0 and d == SUBLANES * LANES
    spread = (jnp.arange(hk)[:, None] == jnp.arange(hk * SUBLANES)[None, :] // SUBLANES).astype(BF16)
    return pl.pallas_call(
        functools.partial(_up_kernel, alpha=alpha),
        grid=(n_tok // tb,),
        in_specs=[
            pl.BlockSpec(memory_space=pl.ANY),
            pl.BlockSpec((tb, hk), lambda i: (i, 0)),
            pl.BlockSpec((tb, d), lambda i: (i, 0)),
            pl.BlockSpec(spread.shape, lambda i: (0, 0)),
            pl.BlockSpec(memory_space=pltpu.VMEM),
            pl.BlockSpec((1, d), lambda i: (0, 0)),
            pl.BlockSpec((1, d), lambda i: (0, 0)),
        ],
        out_specs=pl.BlockSpec((tb, d), lambda i: (i, 0)),
        out_shape=jax.ShapeDtypeStruct((n_tok, d), F32),
        scratch_shapes=_idx_scratch(hk)
        + [pltpu.VMEM((TOK_GROUP * SUBLANES, LANES), F32) for _ in range(2)],
        compiler_params=pltpu.CompilerParams(
            dimension_semantics=("arbitrary",), vmem_limit_bytes=VMEM_LIMIT),
        name="peer_up",
    )(idx, c, xt, spread, tab, ln_g.reshape(1, -1), ln_b.reshape(1, -1))


def _tile(n, want):
    return want if n % want == 0 else n


def kernel(x, mem, w_in, b_in, sgu_g, sgu_b, w_spatial, b_spatial, conv_w, w_mix_out, ln1_g, ln1_b,
           w_xq, w_xkv, w_xo, ln2_g, ln2_b, w_pq, sub_keys, w_down, w_up, ln3_g, ln3_b):
    depth = w_in.shape[0]
    bsz, seq, d = x.shape
    assert d == SUBLANES * LANES and seq % CHUNK == 0
    alpha = np.float32((2 * depth) ** 0.25)
    n_tok = bsz * seq
    ts = _tile(seq, 512)
    for l in range(depth):
        x = _mixer(x, w_in[l], b_in[l], sgu_g[l], sgu_b[l], w_spatial[l], b_spatial[l], conv_w[l],
                   w_mix_out[l], ln1_g[l], ln1_b[l], alpha, ts)
        kv = _kv_proj(mem, w_xkv[l])
        x = _xattn(x, kv, w_xq[l], w_xo[l], ln2_g[l], ln2_b[l], alpha, ts)
        xt = x.reshape(n_tok, d)
        idx, gate = _peer_route(xt, w_pq[l], sub_keys[l], _tile(n_tok, 1024))
        tb = _tile(n_tok, 1024)
        pack = _tile(w_down.shape[1], 512)
        c = _peer_down(idx, xt, gate, _pack_rows(w_down[l], pack), tb)
        x = _peer_up(idx, c, xt, _pack_rows(w_up[l], pack), ln3_g[l], ln3_b[l], alpha, tb)
        x = x.reshape(bsz, seq, d)
    return x
```

```python
import functools

import jax
import jax.numpy as jnp
import numpy as np
from jax import lax
from jax.experimental import pallas as pl
from jax.experimental.pallas import tpu as pltpu

LN_EPS = 1e-5
LANES = 128
SUBLANES = 8
A_HEADS = 4
CHUNK = 128
X_HEADS = 4
PEER_HEADS = 8
N_KEYS = 128
TOPK = 16
ROW_WORDS = 4
VMEM_LIMIT = 56 * 1024 * 1024

F32 = jnp.float32
BF16 = jnp.bfloat16


def _gelu(x):
    return 0.5 * x * (1.0 + lax.erf(x * np.float32(2.0 ** -0.5)))


def _ln(v, g, b):
    mu = jnp.mean(v, axis=-1, keepdims=True)
    d = v - mu
    var = jnp.mean(d * d, axis=-1, keepdims=True)
    return d * lax.rsqrt(var + LN_EPS) * g + b


def _dot(a, b):
    return jnp.dot(a, b, preferred_element_type=F32)


def _dot_nt(a, b):
    return lax.dot_general(a, b, (((1,), (1,)), ((), ())), preferred_element_type=F32)


def _mixer_kernel(x_ref, win_ref, bin_ref, sg_ref, sb_ref, ws_ref, bsp_ref, cw_ref, wout_ref,
                  lg_ref, lb_ref, o_ref, zc_ref, *, alpha, ts, aw):
    @pl.when(pl.program_id(1) == 0)
    def _():
        zc_ref[0:SUBLANES, :] = jnp.zeros((SUBLANES, zc_ref.shape[1]), F32)

    x = x_ref[0]
    xb = x.astype(BF16)

    def zcol(lo, hi):
        return _dot(xb, win_ref[:, lo:hi]) + bin_ref[:, lo:hi]

    hd = aw // A_HEADS
    u = _gelu(zcol(0, aw))
    v = _gelu(zcol(aw, 2 * aw))
    mixed = []
    for h in range(A_HEADS):
        sl = slice(h * hd, (h + 1) * hd)
        vh = _ln(v[:, sl], sg_ref[:, sl], sb_ref[:, sl]).astype(BF16)
        mixed.append(jnp.concatenate(
            [_dot(ws_ref[h], vh[c * CHUNK:(c + 1) * CHUNK]) + bsp_ref[:, sl]
             for c in range(ts // CHUNK)], axis=0))
    ya = u * jnp.concatenate(mixed, axis=-1)

    bw = zc_ref.shape[1]
    hb = zcol(2 * aw, 2 * aw + bw)
    gb = zcol(2 * aw + bw, 2 * aw + 2 * bw)
    gc = zcol(2 * aw + 2 * bw, 2 * aw + 3 * bw)
    zc_ref[SUBLANES:SUBLANES + ts, :] = gc * hb
    conv = (cw_ref[0:1, :] * zc_ref[SUBLANES - 2:SUBLANES - 2 + ts, :]
            + cw_ref[1:2, :] * zc_ref[SUBLANES - 1:SUBLANES - 1 + ts, :]
            + cw_ref[2:3, :] * zc_ref[SUBLANES:SUBLANES + ts, :])
    yb = gb * conv
    zc_ref[0:SUBLANES, :] = zc_ref[ts:ts + SUBLANES, :]

    h = _dot(ya.astype(BF16), wout_ref[0:aw, :]) + _dot(yb.astype(BF16), wout_ref[aw:, :])
    o_ref[0] = _ln(alpha * x + h, lg_ref[...], lb_ref[...])


def _mixer(x, w_in, b_in, sgu_g, sgu_b, w_spatial, b_spatial, conv_w, w_out, ln_g, ln_b, alpha, ts):
    bsz, seq, d = x.shape
    aw = A_HEADS * sgu_g.shape[-1]
    bw = conv_w.shape[-1]
    causal = jnp.tril(jnp.ones((CHUNK, CHUNK), dtype=bool))
    ws = jnp.where(causal[None], w_spatial, 0.0).astype(BF16)
    bsp = jnp.repeat(b_spatial.T, aw // A_HEADS, axis=1)
    const = lambda *shape: pl.BlockSpec(shape, lambda b, i: (0,) * len(shape))
    return pl.pallas_call(
        functools.partial(_mixer_kernel, alpha=alpha, ts=ts, aw=aw),
        grid=(bsz, seq // ts),
        in_specs=[
            pl.BlockSpec((1, ts, d), lambda b, i: (b, i, 0)),
            const(d, w_in.shape[1]), const(1, w_in.shape[1]), const(1, aw), const(1, aw),
            const(A_HEADS, CHUNK, CHUNK), const(CHUNK, aw), const(conv_w.shape[0], bw),
            const(aw + bw, d), const(1, d), const(1, d),
        ],
        out_specs=pl.BlockSpec((1, ts, d), lambda b, i: (b, i, 0)),
        out_shape=jax.ShapeDtypeStruct(x.shape, F32),
        scratch_shapes=[pltpu.VMEM((ts + 2 * SUBLANES, bw), F32)],
        compiler_params=pltpu.CompilerParams(
            dimension_semantics=("parallel", "arbitrary"), vmem_limit_bytes=VMEM_LIMIT),
        name="mixer",
    )(x, w_in.astype(BF16), b_in.reshape(1, -1), sgu_g.reshape(1, -1), sgu_b.reshape(1, -1),
      ws, bsp, conv_w.reshape(conv_w.shape[0], bw), w_out.astype(BF16),
      ln_g.reshape(1, -1), ln_b.reshape(1, -1))


def _kv_kernel(m_ref, w_ref, o_ref):
    o_ref[0] = _dot(m_ref[0].astype(BF16), w_ref[...]).astype(BF16)


def _kv_proj(mem, w_kv):
    bsz, m, d = mem.shape
    n = w_kv.shape[1]
    return pl.pallas_call(
        _kv_kernel,
        grid=(bsz,),
        in_specs=[pl.BlockSpec((1, m, d), lambda b: (b, 0, 0)),
                  pl.BlockSpec((d, n), lambda b: (0, 0))],
        out_specs=pl.BlockSpec((1, m, n), lambda b: (b, 0, 0)),
        out_shape=jax.ShapeDtypeStruct((bsz, m, n), BF16),
        compiler_params=pltpu.CompilerParams(
            dimension_semantics=("parallel",), vmem_limit_bytes=VMEM_LIMIT),
        name="kv_proj",
    )(mem, w_kv.astype(BF16))


def _xattn_kernel(x_ref, kv_ref, wq_ref, wo_ref, lg_ref, lb_ref, o_ref, *, alpha):
    x = x_ref[0]
    d = x.shape[-1]
    dk = d // X_HEADS
    q = _dot(x.astype(BF16), wq_ref[...])
    outs = []
    for h in range(X_HEADS):
        qh = q[:, h * dk:(h + 1) * dk].astype(BF16)
        s = _dot_nt(qh, kv_ref[0, :, h * dk:(h + 1) * dk]) * np.float32(dk ** -0.5)
        e = jnp.exp(s - jnp.max(s, axis=-1, keepdims=True))
        p = e / jnp.sum(e, axis=-1, keepdims=True)
        outs.append(_dot(p.astype(BF16), kv_ref[0, :, d + h * dk:d + (h + 1) * dk]))
    o = jnp.concatenate(outs, axis=-1).astype(BF16)
    o_ref[0] = _ln(alpha * x + _dot(o, wo_ref[...]), lg_ref[...], lb_ref[...])


def _xattn(x, kv, w_q, w_o, ln_g, ln_b, alpha, ts):
    bsz, seq, d = x.shape
    m = kv.shape[1]
    return pl.pallas_call(
        functools.partial(_xattn_kernel, alpha=alpha),
        grid=(bsz, seq // ts),
        in_specs=[
            pl.BlockSpec((1, ts, d), lambda b, i: (b, i, 0)),
            pl.BlockSpec((1, m, 2 * d), lambda b, i: (b, 0, 0)),
            pl.BlockSpec((d, d), lambda b, i: (0, 0)),
            pl.BlockSpec((d, d), lambda b, i: (0, 0)),
            pl.BlockSpec((1, d), lambda b, i: (0, 0)),
            pl.BlockSpec((1, d), lambda b, i: (0, 0)),
        ],
        out_specs=pl.BlockSpec((1, ts, d), lambda b, i: (b, i, 0)),
        out_shape=jax.ShapeDtypeStruct(x.shape, F32),
        compiler_params=pltpu.CompilerParams(
            dimension_semantics=("parallel", "parallel"), vmem_limit_bytes=VMEM_LIMIT),
        name="xattn",
    )(x, kv, w_q.astype(BF16), w_o.astype(BF16), ln_g.reshape(1, -1), ln_b.reshape(1, -1))


def _top_pairs(va, vb, ea, eb, rounds):
    assert rounds == 2 * SUBLANES
    lanes = va.shape[1]
    sub = lax.broadcasted_iota(jnp.int32, (SUBLANES, lanes), 0).astype(F32)
    big = np.float32(2 ** 24)
    v, e = [], []
    for i in range(rounds):
        col = va[i:i + 1] + vb[0:SUBLANES]
        live = rounds // (i + 1)
        v.append(col if live >= SUBLANES else jnp.where(sub < np.float32(live), col, -jnp.inf))
        e.append(ea[i:i + 1] + eb[0:SUBLANES])
    v_hi = va[0:1] + vb[SUBLANES:]
    e_hi = ea[0:1] + eb[SUBLANES:]
    code = sub
    code_hi = sub + np.float32(SUBLANES)
    tops, picks = [], []
    for r in range(rounds):
        m = jnp.max(jnp.maximum(v[0], v_hi), axis=0, keepdims=True)
        first = jnp.min(jnp.minimum(jnp.where(v[0] == m, code, big), jnp.where(v_hi == m, code_hi, big)),
                        axis=0, keepdims=True)
        hit = code == first
        hit_hi = code_hi == first
        tops.append(m)
        picks.append(jnp.max(jnp.maximum(jnp.where(hit, e[0], -1.0), jnp.where(hit_hi, e_hi, -1.0)),
                             axis=0, keepdims=True))
        v_hi = jnp.where(hit_hi, -jnp.inf, v_hi)
        code = jnp.where(hit, code + np.float32(rounds), code)
        for i in range(rounds - 1 - r):
            v[i] = jnp.where(hit, v[i + 1], v[i])
            e[i] = jnp.where(hit, e[i + 1], e[i])
    return jnp.concatenate(tops, axis=0), jnp.concatenate(picks, axis=0)


def _merge_sort_network(n):
    def merge(lo, hi, r):
        step = 2 * r
        if step < hi - lo:
            yield from merge(lo, hi, step)
            yield from merge(lo + r, hi, step)
            yield from ((i, i + r) for i in range(lo + r, hi - r, step))
        else:
            yield lo, lo + r

    def sort(lo, hi):
        if hi > lo:
            mid = lo + (hi - lo) // 2
            yield from sort(lo, mid)
            yield from sort(mid + 1, hi)
            yield from merge(lo, hi, 1)

    assert n & (n - 1) == 0
    return list(sort(0, n - 1))


def _top_rows(vals, rounds):
    n, lanes = vals.shape
    nt = n // SUBLANES
    assert rounds <= nt
    sub = lax.broadcasted_iota(jnp.int32, (SUBLANES, lanes), 0).astype(F32)
    v = [vals[k * SUBLANES:(k + 1) * SUBLANES] for k in range(nt)]
    idx = [sub + np.float32(SUBLANES * k) for k in range(nt)]
    for a, b in _merge_sort_network(nt):
        swap = (v[b] > v[a]) | ((v[b] == v[a]) & (idx[b] < idx[a]))
        v[a], v[b] = jnp.maximum(v[a], v[b]), jnp.minimum(v[a], v[b])
        idx[a], idx[b] = jnp.where(swap, idx[b], idx[a]), jnp.where(swap, idx[a], idx[b])
    tops, picks = [], []
    for r in range(rounds):
        m = jnp.max(v[0], axis=0, keepdims=True)
        first = jnp.min(jnp.where(v[0] == m, idx[0], np.float32(2 ** 24)), axis=0, keepdims=True)
        tops.append(m)
        picks.append(first)
        hit = idx[0] == first
        for k in range(rounds - 1 - r):
            v[k] = jnp.where(hit, v[k + 1], v[k])
            idx[k] = jnp.where(hit, idx[k + 1], idx[k])
    return jnp.concatenate(tops, axis=0), jnp.concatenate(picks, axis=0)


def _route_kernel(x_ref, wq_ref, keys_ref, idx_ref, gate_ref, idx_t, gate_t, sc_ref, top_ref):
    xb = x_ref[...].astype(BF16)
    q = _dot(xb, wq_ref[...]).astype(BF16)
    hk = q.shape[1] // (2 * PEER_HEADS)
    for i in range(2 * PEER_HEADS):
        sc_ref[i] = _dot_nt(keys_ref[i // 2, i % 2], q[:, i * hk:(i + 1) * hk])
    top_ref[1] = jnp.zeros(top_ref.shape[1:], F32)

    def sub_top(h, slot):
        va, ia = _top_rows(sc_ref[2 * h], TOPK)
        vb, ib = _top_rows(sc_ref[2 * h + 1], TOPK)
        top_ref[slot] = jnp.concatenate([va, vb, ia, ib], axis=0)

    def pair_top(h, slot):
        t = top_ref[slot]
        va, vb, ia, ib = (t[i * TOPK:(i + 1) * TOPK] for i in range(4))
        top, expert = _top_pairs(va, vb, ia * np.float32(N_KEYS), ib, TOPK)
        e = jnp.exp(top - top[0:1])
        r0 = pl.multiple_of(h * TOPK, TOPK)
        gate_t[pl.ds(r0, TOPK), :] = e / jnp.sum(e, axis=0, keepdims=True)
        idx_t[pl.ds(r0, TOPK), :] = expert.astype(jnp.int32) * ROW_WORDS

    def two_heads(p, carry):
        h = 2 * p
        sub_top(h, 0)
        pair_top(jnp.maximum(h - 1, 0), 1)
        sub_top(h + 1, 1)
        pair_top(h, 0)
        return carry

    lax.fori_loop(0, PEER_HEADS // 2, two_heads, 0)
    pair_top(PEER_HEADS - 1, 1)
    idx_ref[...] = idx_t[...].T
    gate_ref[...] = gate_t[...].T


def _peer_route(xt, w_query, sub_keys, tt):
    n_tok, d = xt.shape
    wq = w_query.astype(BF16)
    hk = PEER_HEADS * TOPK
    return pl.pallas_call(
        _route_kernel,
        grid=(n_tok // tt,),
        in_specs=[
            pl.BlockSpec((tt, d), lambda i: (i, 0)),
            pl.BlockSpec(wq.shape, lambda i: (0, 0)),
            pl.BlockSpec(sub_keys.shape, lambda i: (0, 0, 0, 0)),
        ],
        out_specs=[pl.BlockSpec((tt, hk), lambda i: (i, 0)),
                   pl.BlockSpec((tt, hk), lambda i: (i, 0))],
        out_shape=[jax.ShapeDtypeStruct((n_tok, hk), jnp.int32),
                   jax.ShapeDtypeStruct((n_tok, hk), F32)],
        scratch_shapes=[pltpu.VMEM((hk, tt), jnp.int32), pltpu.VMEM((hk, tt), F32),
                        pltpu.VMEM((2 * PEER_HEADS, N_KEYS, tt), F32), pltpu.VMEM((2, 4 * TOPK, tt), F32)],
        compiler_params=pltpu.CompilerParams(
            dimension_semantics=("parallel",), vmem_limit_bytes=VMEM_LIMIT),
        name="peer_route",
    )(xt, wq, sub_keys.astype(BF16))


TOK_GROUP = 128


def _pack_kernel(w_ref, o_ref):
    n = w_ref.shape[0]
    for s in range(ROW_WORDS):
        lo = w_ref[:, 2 * s * LANES:(2 * s + 1) * LANES].astype(BF16).astype(F32)
        hi = w_ref[:, (2 * s + 1) * LANES:(2 * s + 2) * LANES].astype(BF16).astype(F32)
        word = pltpu.bitcast(hi, jnp.uint32) | (pltpu.bitcast(lo, jnp.uint32) >> 16)
        o_ref[pl.ds(s, n, stride=ROW_WORDS), :] = word


def _pack_rows(w, rows):
    n, d = w.shape
    assert d == 2 * ROW_WORDS * LANES and n % rows == 0
    return pl.pallas_call(
        _pack_kernel,
        grid=(n // rows,),
        in_specs=[pl.BlockSpec((rows, d), lambda i: (i, 0))],
        out_specs=pl.BlockSpec((rows * ROW_WORDS, LANES), lambda i: (i, 0)),
        out_shape=jax.ShapeDtypeStruct((n * ROW_WORDS, LANES), jnp.uint32),
        compiler_params=pltpu.CompilerParams(
            dimension_semantics=("parallel",), vmem_limit_bytes=VMEM_LIMIT),
        name="pack_rows",
    )(w)


def _gather_rows(idx_ref, t, tab_ref, hk):
    words = [tab_ref[pl.ds(pl.multiple_of(idx_ref[t, k], ROW_WORDS), ROW_WORDS), :] for k in range(hk)]
    return pltpu.bitcast(jnp.concatenate(words, axis=0), BF16)


def _diag_mask(hk):
    r = lax.broadcasted_iota(jnp.int32, (SUBLANES, hk * SUBLANES), 0)
    c = lax.broadcasted_iota(jnp.int32, (SUBLANES, hk * SUBLANES), 1)
    return (c & (SUBLANES - 1)) == r


def _idx_copy(idx_hbm, row0, buf, sem):
    return pltpu.make_async_copy(idx_hbm.at[pl.ds(row0, TOK_GROUP)], buf, sem)


def _for_each_group(idx_hbm, tb, idx_a, idx_b, sems, process):
    step = pl.program_id(0)
    base = step * tb
    last = idx_hbm.shape[0] - TOK_GROUP

    @pl.when(step == 0)
    def _():
        _idx_copy(idx_hbm, 0, idx_a, sems.at[0]).start()

    def pair(i, carry):
        t0 = pl.multiple_of(i * (2 * TOK_GROUP), 2 * TOK_GROUP)
        _idx_copy(idx_hbm, base + t0 + TOK_GROUP, idx_b, sems.at[1]).start()
        _idx_copy(idx_hbm, 0, idx_a, sems.at[0]).wait()
        process(idx_a, t0, 0)
        _idx_copy(idx_hbm, jnp.minimum(base + t0 + 2 * TOK_GROUP, last), idx_a, sems.at[0]).start()
        _idx_copy(idx_hbm, 0, idx_b, sems.at[1]).wait()
        process(idx_b, t0 + TOK_GROUP, 1)
        return carry

    lax.fori_loop(0, tb // (2 * TOK_GROUP), pair, 0)

    @pl.when(step == pl.num_programs(0) - 1)
    def _():
        _idx_copy(idx_hbm, 0, idx_a, sems.at[0]).wait()


def _idx_scratch(hk):
    return [pltpu.SMEM((TOK_GROUP, hk), jnp.int32), pltpu.SMEM((TOK_GROUP, hk), jnp.int32),
            pltpu.SemaphoreType.DMA((2,))]


def _down_kernel(idx_hbm, x_ref, gate_ref, fold_ref, tab_ref, c_ref, idx_a, idx_b, sems, xt_ref,
                 part_a, part_b):
    hk = gate_ref.shape[1]
    tb = x_ref.shape[0]
    mask = _diag_mask(hk)
    parts = (part_a, part_b)

    def finish(part_ref, t0):
        z = _dot(part_ref[...].astype(BF16), fold_ref[...])
        act = jnp.concatenate(
            [jnp.sum(z[j * SUBLANES:(j + 1) * SUBLANES], axis=0, keepdims=True)
             for j in range(TOK_GROUP)], axis=0)
        c_ref[pl.ds(t0, TOK_GROUP), :] = gate_ref[pl.ds(t0, TOK_GROUP), :] * _gelu(act)

    def process(idx_ref, t0, slot):
        finish(parts[1 - slot], pl.multiple_of(jnp.maximum(t0 - TOK_GROUP, 0), TOK_GROUP))
        xs = x_ref[pl.ds(t0, TOK_GROUP), :]
        for r in range(SUBLANES):
            xt_ref[pl.ds(r, TOK_GROUP, stride=SUBLANES), :] = xs[:, r * LANES:(r + 1) * LANES]
        for j in range(TOK_GROUP):
            rows = _gather_rows(idx_ref, j, tab_ref, hk)
            xj = xt_ref[j * SUBLANES:(j + 1) * SUBLANES, :].astype(BF16)
            s = _dot_nt(xj, rows)
            parts[slot][j * SUBLANES:(j + 1) * SUBLANES, :] = jnp.where(mask, s, 0.0)

    part_b[...] = jnp.zeros(part_b.shape, F32)
    _for_each_group(idx_hbm, tb, idx_a, idx_b, sems, process)
    finish(part_b, tb - TOK_GROUP)


def _peer_down(idx, xt, gate, tab, tb):
    n_tok, hk = idx.shape
    assert tb % (2 * TOK_GROUP) == 0 and xt.shape[1] == SUBLANES * LANES
    fold = (jnp.arange(hk * SUBLANES)[:, None] // SUBLANES == jnp.arange(hk)[None, :]).astype(BF16)
    return pl.pallas_call(
        _down_kernel,
        grid=(n_tok // tb,),
        in_specs=[
            pl.BlockSpec(memory_space=pl.ANY),
            pl.BlockSpec((tb, SUBLANES * LANES), lambda i: (i, 0)),
            pl.BlockSpec((tb, hk), lambda i: (i, 0)),
            pl.BlockSpec(fold.shape, lambda i: (0, 0)),
            pl.BlockSpec(memory_space=pltpu.VMEM),
        ],
        out_specs=pl.BlockSpec((tb, hk), lambda i: (i, 0)),
        out_shape=jax.ShapeDtypeStruct((n_tok, hk), F32),
        scratch_shapes=_idx_scratch(hk) + [pltpu.VMEM((TOK_GROUP * SUBLANES, LANES), F32)]
        + [pltpu.VMEM((TOK_GROUP * SUBLANES, hk * SUBLANES), F32) for _ in range(2)],
        compiler_params=pltpu.CompilerParams(
            dimension_semantics=("arbitrary",), vmem_limit_bytes=VMEM_LIMIT),
        name="peer_down",
    )(idx, xt, gate, fold, tab)


def _up_kernel(idx_hbm, c_ref, x_ref, spread_ref, tab_ref, g_ref, b_ref, o_ref, idx_a, idx_b, sems,
               ht_a, ht_b, *, alpha):
    hk = c_ref.shape[1]
    tb = c_ref.shape[0]
    mask = _diag_mask(hk)
    hts = (ht_a, ht_b)

    def finish(ht_ref, t0):
        h = jnp.concatenate([ht_ref[pl.ds(r, TOK_GROUP, stride=SUBLANES), :] for r in range(SUBLANES)],
                            axis=-1)
        o_ref[pl.ds(t0, TOK_GROUP), :] = _ln(alpha * x_ref[pl.ds(t0, TOK_GROUP), :] + h,
                                              g_ref[...], b_ref[...])

    def process(idx_ref, t0, slot):
        finish(hts[1 - slot], pl.multiple_of(jnp.maximum(t0 - TOK_GROUP, 0), TOK_GROUP))
        cexp = _dot(c_ref[pl.ds(t0, TOK_GROUP), :].astype(BF16), spread_ref[...])
        for j in range(TOK_GROUP):
            rows = _gather_rows(idx_ref, j, tab_ref, hk)
            cj = jnp.where(mask, jnp.broadcast_to(cexp[j:j + 1, :], mask.shape), 0.0)
            hts[slot][j * SUBLANES:(j + 1) * SUBLANES, :] = _dot(cj.astype(BF16), rows)

    ht_b[...] = jnp.zeros(ht_b.shape, F32)
    _for_each_group(idx_hbm, tb, idx_a, idx_b, sems, process)
    finish(ht_b, tb - TOK_GROUP)


def _peer_up(idx, c, xt, tab, ln_g, ln_b, alpha, tb):
    n_tok, hk = idx.shape
    d = xt.shape[1]
    assert tb % (2 * TOK_GROUP) == 0 and d == SUBLANES * LANES
    spread = (jnp.arange(hk)[:, None] == jnp.arange(hk * SUBLANES)[None, :] // SUBLANES).astype(BF16)
    return pl.pallas_call(
        functools.partial(_up_kernel, alpha=alpha),
        grid=(n_tok // tb,),
        in_specs=[
            pl.BlockSpec(memory_space=pl.ANY),
            pl.BlockSpec((tb, hk), lambda i: (i, 0)),
            pl.BlockSpec((tb, d), lambda i: (i, 0)),
            pl.BlockSpec(spread.shape, lambda i: (0, 0)),
            pl.BlockSpec(memory_space=pltpu.VMEM),
            pl.BlockSpec((1, d), lambda i: (0, 0)),
            pl.BlockSpec((1, d), lambda i: (0, 0)),
        ],
        out_specs=pl.BlockSpec((tb, d), lambda i: (i, 0)),
        out_shape=jax.ShapeDtypeStruct((n_tok, d), F32),
        scratch_shapes=_idx_scratch(hk)
        + [pltpu.VMEM((TOK_GROUP * SUBLANES, LANES), F32) for _ in range(2)],
        compiler_params=pltpu.CompilerParams(
            dimension_semantics=("arbitrary",), vmem_limit_bytes=VMEM_LIMIT),
        name="peer_up",
    )(idx, c, xt, spread, tab, ln_g.reshape(1, -1), ln_b.reshape(1, -1))


def _tile(n, want):
    return want if n % want == 0 else n


def kernel(x, mem, w_in, b_in, sgu_g, sgu_b, w_spatial, b_spatial, conv_w, w_mix_out, ln1_g, ln1_b,
           w_xq, w_xkv, w_xo, ln2_g, ln2_b, w_pq, sub_keys, w_down, w_up, ln3_g, ln3_b):
    depth = w_in.shape[0]
    bsz, seq, d = x.shape
    assert d == SUBLANES * LANES and seq % CHUNK == 0
    alpha = np.float32((2 * depth) ** 0.25)
    n_tok = bsz * seq
    ts = _tile(seq, 512)
    for l in range(depth):
        x = _mixer(x, w_in[l], b_in[l], sgu_g[l], sgu_b[l], w_spatial[l], b_spatial[l], conv_w[l],
                   w_mix_out[l], ln1_g[l], ln1_b[l], alpha, ts)
        kv = _kv_proj(mem, w_xkv[l])
        x = _xattn(x, kv, w_xq[l], w_xo[l], ln2_g[l], ln2_b[l], alpha, ts)
        xt = x.reshape(n_tok, d)
        idx, gate = _peer_route(xt, w_pq[l], sub_keys[l], _tile(n_tok, 1024))
        tb = _tile(n_tok, 1024)
        pack = _tile(w_down.shape[1], 512)
        c = _peer_down(idx, xt, gate, _pack_rows(w_down[l], pack), tb)
        x = _peer_up(idx, c, xt, _pack_rows(w_up[l], pack), ln3_g[l], ln3_b[l], alpha, tb)
        x = x.reshape(bsz, seq, d)
    return x
```

```python
import functools

import jax
import jax.numpy as jnp
import numpy as np
from jax import lax
from jax.experimental import pallas as pl
from jax.experimental.pallas import tpu as pltpu

LN_EPS = 1e-5
LANES = 128
SUBLANES = 8
A_HEADS = 4
CHUNK = 128
X_HEADS = 4
PEER_HEADS = 8
N_KEYS = 128
TOPK = 16
ROW_WORDS = 4
VMEM_LIMIT = 56 * 1024 * 1024

F32 = jnp.float32
BF16 = jnp.bfloat16


def _gelu(x):
    return 0.5 * x * (1.0 + lax.erf(x * np.float32(2.0 ** -0.5)))


def _ln(v, g, b):
    mu = jnp.mean(v, axis=-1, keepdims=True)
    d = v - mu
    var = jnp.mean(d * d, axis=-1, keepdims=True)
    return d * lax.rsqrt(var + LN_EPS) * g + b


def _dot(a, b):
    return jnp.dot(a, b, preferred_element_type=F32)


def _dot_nt(a, b):
    return lax.dot_general(a, b, (((1,), (1,)), ((), ())), preferred_element_type=F32)


def _mixer_kernel(x_ref, win_ref, bin_ref, sg_ref, sb_ref, ws_ref, bsp_ref, cw_ref, wout_ref,
                  lg_ref, lb_ref, o_ref, zc_ref, *, alpha, ts, aw):
    @pl.when(pl.program_id(1) == 0)
    def _():
        zc_ref[0:SUBLANES, :] = jnp.zeros((SUBLANES, zc_ref.shape[1]), F32)

    x = x_ref[0]
    xb = x.astype(BF16)

    def zcol(lo, hi):
        return _dot(xb, win_ref[:, lo:hi]) + bin_ref[:, lo:hi]

    hd = aw // A_HEADS
    u = _gelu(zcol(0, aw))
    v = _gelu(zcol(aw, 2 * aw))
    mixed = []
    for h in range(A_HEADS):
        sl = slice(h * hd, (h + 1) * hd)
        vh = _ln(v[:, sl], sg_ref[:, sl], sb_ref[:, sl]).astype(BF16)
        mixed.append(jnp.concatenate(
            [_dot(ws_ref[h], vh[c * CHUNK:(c + 1) * CHUNK]) + bsp_ref[:, sl]
             for c in range(ts // CHUNK)], axis=0))
    ya = u * jnp.concatenate(mixed, axis=-1)

    bw = zc_ref.shape[1]
    hb = zcol(2 * aw, 2 * aw + bw)
    gb = zcol(2 * aw + bw, 2 * aw + 2 * bw)
    gc = zcol(2 * aw + 2 * bw, 2 * aw + 3 * bw)
    zc_ref[SUBLANES:SUBLANES + ts, :] = gc * hb
    conv = (cw_ref[0:1, :] * zc_ref[SUBLANES - 2:SUBLANES - 2 + ts, :]
            + cw_ref[1:2, :] * zc_ref[SUBLANES - 1:SUBLANES - 1 + ts, :]
            + cw_ref[2:3, :] * zc_ref[SUBLANES:SUBLANES + ts, :])
    yb = gb * conv
    zc_ref[0:SUBLANES, :] = zc_ref[ts:ts + SUBLANES, :]

    h = _dot(ya.astype(BF16), wout_ref[0:aw, :]) + _dot(yb.astype(BF16), wout_ref[aw:, :])
    o_ref[0] = _ln(alpha * x + h, lg_ref[...], lb_ref[...])


def _mixer(x, w_in, b_in, sgu_g, sgu_b, w_spatial, b_spatial, conv_w, w_out, ln_g, ln_b, alpha, ts):
    bsz, seq, d = x.shape
    aw = A_HEADS * sgu_g.shape[-1]
    bw = conv_w.shape[-1]
    causal = jnp.tril(jnp.ones((CHUNK, CHUNK), dtype=bool))
    ws = jnp.where(causal[None], w_spatial, 0.0).astype(BF16)
    bsp = jnp.repeat(b_spatial.T, aw // A_HEADS, axis=1)
    const = lambda *shape: pl.BlockSpec(shape, lambda b, i: (0,) * len(shape))
    return pl.pallas_call(
        functools.partial(_mixer_kernel, alpha=alpha, ts=ts, aw=aw),
        grid=(bsz, seq // ts),
        in_specs=[
            pl.BlockSpec((1, ts, d), lambda b, i: (b, i, 0)),
            const(d, w_in.shape[1]), const(1, w_in.shape[1]), const(1, aw), const(1, aw),
            const(A_HEADS, CHUNK, CHUNK), const(CHUNK, aw), const(conv_w.shape[0], bw),
            const(aw + bw, d), const(1, d), const(1, d),
        ],
        out_specs=pl.BlockSpec((1, ts, d), lambda b, i: (b, i, 0)),
        out_shape=jax.ShapeDtypeStruct(x.shape, F32),
        scratch_shapes=[pltpu.VMEM((ts + 2 * SUBLANES, bw), F32)],
        compiler_params=pltpu.CompilerParams(
            dimension_semantics=("parallel", "arbitrary"), vmem_limit_bytes=VMEM_LIMIT),
        name="mixer",
    )(x, w_in.astype(BF16), b_in.reshape(1, -1), sgu_g.reshape(1, -1), sgu_b.reshape(1, -1),
      ws, bsp, conv_w.reshape(conv_w.shape[0], bw), w_out.astype(BF16),
      ln_g.reshape(1, -1), ln_b.reshape(1, -1))


def _kv_kernel(m_ref, w_ref, o_ref):
    o_ref[0] = _dot(m_ref[0].astype(BF16), w_ref[...]).astype(BF16)


def _kv_proj(mem, w_kv):
    bsz, m, d = mem.shape
    n = w_kv.shape[1]
    return pl.pallas_call(
        _kv_kernel,
        grid=(bsz,),
        in_specs=[pl.BlockSpec((1, m, d), lambda b: (b, 0, 0)),
                  pl.BlockSpec((d, n), lambda b: (0, 0))],
        out_specs=pl.BlockSpec((1, m, n), lambda b: (b, 0, 0)),
        out_shape=jax.ShapeDtypeStruct((bsz, m, n), BF16),
        compiler_params=pltpu.CompilerParams(
            dimension_semantics=("parallel",), vmem_limit_bytes=VMEM_LIMIT),
        name="kv_proj",
    )(mem, w_kv.astype(BF16))


def _xattn_kernel(x_ref, kv_ref, wq_ref, wo_ref, lg_ref, lb_ref, o_ref, *, alpha):
    x = x_ref[0]
    d = x.shape[-1]
    dk = d // X_HEADS
    q = _dot(x.astype(BF16), wq_ref[...])
    outs = []
    for h in range(X_HEADS):
        qh = q[:, h * dk:(h + 1) * dk].astype(BF16)
        s = _dot_nt(qh, kv_ref[0, :, h * dk:(h + 1) * dk]) * np.float32(dk ** -0.5)
        e = jnp.exp(s - jnp.max(s, axis=-1, keepdims=True))
        p = e / jnp.sum(e, axis=-1, keepdims=True)
        outs.append(_dot(p.astype(BF16), kv_ref[0, :, d + h * dk:d + (h + 1) * dk]))
    o = jnp.concatenate(outs, axis=-1).astype(BF16)
    o_ref[0] = _ln(alpha * x + _dot(o, wo_ref[...]), lg_ref[...], lb_ref[...])


def _xattn(x, kv, w_q, w_o, ln_g, ln_b, alpha, ts):
    bsz, seq, d = x.shape
    m = kv.shape[1]
    return pl.pallas_call(
        functools.partial(_xattn_kernel, alpha=alpha),
        grid=(bsz, seq // ts),
        in_specs=[
            pl.BlockSpec((1, ts, d), lambda b, i: (b, i, 0)),
            pl.BlockSpec((1, m, 2 * d), lambda b, i: (b, 0, 0)),
            pl.BlockSpec((d, d), lambda b, i: (0, 0)),
            pl.BlockSpec((d, d), lambda b, i: (0, 0)),
            pl.BlockSpec((1, d), lambda b, i: (0, 0)),
            pl.BlockSpec((1, d), lambda b, i: (0, 0)),
        ],
        out_specs=pl.BlockSpec((1, ts, d), lambda b, i: (b, i, 0)),
        out_shape=jax.ShapeDtypeStruct(x.shape, F32),
        compiler_params=pltpu.CompilerParams(
            dimension_semantics=("parallel", "parallel"), vmem_limit_bytes=VMEM_LIMIT),
        name="xattn",
    )(x, kv, w_q.astype(BF16), w_o.astype(BF16), ln_g.reshape(1, -1), ln_b.reshape(1, -1))


def _top_pairs(va, vb, ea, eb, rounds):
    assert rounds == 2 * SUBLANES
    lanes = va.shape[1]
    sub = lax.broadcasted_iota(jnp.int32, (SUBLANES, lanes), 0).astype(F32)
    big = np.float32(2 ** 24)
    v, e = [], []
    for i in range(rounds):
        col = va[i:i + 1] + vb[0:SUBLANES]
        live = rounds // (i + 1)
        v.append(col if live >= SUBLANES else jnp.where(sub < np.float32(live), col, -jnp.inf))
        e.append(ea[i:i + 1] + eb[0:SUBLANES])
    v_hi = va[0:1] + vb[SUBLANES:]
    e_hi = ea[0:1] + eb[SUBLANES:]
    code = sub
    code_hi = sub + np.float32(SUBLANES)
    tops, picks = [], []
    for r in range(rounds):
        m = jnp.max(jnp.maximum(v[0], v_hi), axis=0, keepdims=True)
        first = jnp.min(jnp.minimum(jnp.where(v[0] == m, code, big), jnp.where(v_hi == m, code_hi, big)),
                        axis=0, keepdims=True)
        hit = code == first
        hit_hi = code_hi == first
        tops.append(m)
        picks.append(jnp.max(jnp.maximum(jnp.where(hit, e[0], -1.0), jnp.where(hit_hi, e_hi, -1.0)),
                             axis=0, keepdims=True))
        v_hi = jnp.where(hit_hi, -jnp.inf, v_hi)
        code = jnp.where(hit, code + np.float32(rounds), code)
        for i in range(rounds - 1 - r):
            v[i] = jnp.where(hit, v[i + 1], v[i])
            e[i] = jnp.where(hit, e[i + 1], e[i])
    return jnp.concatenate(tops, axis=0), jnp.concatenate(picks, axis=0)


def _merge_sort_network(n):
    def merge(lo, hi, r):
        step = 2 * r
        if step < hi - lo:
            yield from merge(lo, hi, step)
            yield from merge(lo + r, hi, step)
            yield from ((i, i + r) for i in range(lo + r, hi - r, step))
        else:
            yield lo, lo + r

    def sort(lo, hi):
        if hi > lo:
            mid = lo + (hi - lo) // 2
            yield from sort(lo, mid)
            yield from sort(mid + 1, hi)
            yield from merge(lo, hi, 1)

    assert n & (n - 1) == 0
    return list(sort(0, n - 1))


def _top_rows(vals, rounds):
    n, lanes = vals.shape
    nt = n // SUBLANES
    assert rounds <= nt
    sub = lax.broadcasted_iota(jnp.int32, (SUBLANES, lanes), 0).astype(F32)
    v = [vals[k * SUBLANES:(k + 1) * SUBLANES] for k in range(nt)]
    idx = [sub + np.float32(SUBLANES * k) for k in range(nt)]
    for a, b in _merge_sort_network(nt):
        swap = (v[b] > v[a]) | ((v[b] == v[a]) & (idx[b] < idx[a]))
        v[a], v[b] = jnp.maximum(v[a], v[b]), jnp.minimum(v[a], v[b])
        idx[a], idx[b] = jnp.where(swap, idx[b], idx[a]), jnp.where(swap, idx[a], idx[b])
    tops, picks = [], []
    for r in range(rounds):
        m = jnp.max(v[0], axis=0, keepdims=True)
        first = jnp.min(jnp.where(v[0] == m, idx[0], np.float32(2 ** 24)), axis=0, keepdims=True)
        tops.append(m)
        picks.append(first)
        hit = idx[0] == first
        for k in range(rounds - 1 - r):
            v[k] = jnp.where(hit, v[k + 1], v[k])
            idx[k] = jnp.where(hit, idx[k + 1], idx[k])
    return jnp.concatenate(tops, axis=0), jnp.concatenate(picks, axis=0)


def _fold_keys_kernel(w_ref, k_ref, o_ref):
    o_ref[...] = lax.dot_general(k_ref[0], w_ref[...], (((1,), (1,)), ((), ())),
                                 preferred_element_type=F32, precision=lax.Precision.HIGHEST).astype(BF16)


def _fold_keys(w_query, sub_keys):
    d = w_query.shape[0]
    half_key = sub_keys.shape[-1]
    keys = sub_keys.reshape(-1, N_KEYS, half_key)
    return pl.pallas_call(
        _fold_keys_kernel,
        grid=(keys.shape[0],),
        in_specs=[pl.BlockSpec((d, half_key), lambda i: (0, i)),
                  pl.BlockSpec((1, N_KEYS, half_key), lambda i: (i, 0, 0))],
        out_specs=pl.BlockSpec((N_KEYS, d), lambda i: (i, 0)),
        out_shape=jax.ShapeDtypeStruct((keys.shape[0] * N_KEYS, d), BF16),
        compiler_params=pltpu.CompilerParams(
            dimension_semantics=("parallel",), vmem_limit_bytes=VMEM_LIMIT),
        name="fold_keys",
    )(w_query, keys)


def _route_kernel(x_ref, wk_ref, idx_ref, gate_ref, idx_t, gate_t, sc_ref, top_ref):
    sc = _dot_nt(wk_ref[...], x_ref[...].astype(BF16))
    for i in range(2 * PEER_HEADS):
        sc_ref[i] = sc[i * N_KEYS:(i + 1) * N_KEYS]
    top_ref[1] = jnp.zeros(top_ref.shape[1:], F32)

    def sub_top(h, slot):
        va, ia = _top_rows(sc_ref[2 * h], TOPK)
        vb, ib = _top_rows(sc_ref[2 * h + 1], TOPK)
        top_ref[slot] = jnp.concatenate([va, vb, ia, ib], axis=0)

    def pair_top(h, slot):
        t = top_ref[slot]
        va, vb, ia, ib = (t[i * TOPK:(i + 1) * TOPK] for i in range(4))
        top, expert = _top_pairs(va, vb, ia * np.float32(N_KEYS), ib, TOPK)
        e = jnp.exp(top - top[0:1])
        r0 = pl.multiple_of(h * TOPK, TOPK)
        gate_t[pl.ds(r0, TOPK), :] = e / jnp.sum(e, axis=0, keepdims=True)
        idx_t[pl.ds(r0, TOPK), :] = expert.astype(jnp.int32) * ROW_WORDS

    def two_heads(p, carry):
        h = 2 * p
        sub_top(h, 0)
        pair_top(jnp.maximum(h - 1, 0), 1)
        sub_top(h + 1, 1)
        pair_top(h, 0)
        return carry

    lax.fori_loop(0, PEER_HEADS // 2, two_heads, 0)
    pair_top(PEER_HEADS - 1, 1)
    idx_ref[...] = idx_t[...].T
    gate_ref[...] = gate_t[...].T


def _peer_route(xt, w_query, sub_keys, tt):
    n_tok, d = xt.shape
    wk = _fold_keys(w_query, sub_keys)
    hk = PEER_HEADS * TOPK
    return pl.pallas_call(
        _route_kernel,
        grid=(n_tok // tt,),
        in_specs=[
            pl.BlockSpec((tt, d), lambda i: (i, 0)),
            pl.BlockSpec(wk.shape, lambda i: (0, 0)),
        ],
        out_specs=[pl.BlockSpec((tt, hk), lambda i: (i, 0)),
                   pl.BlockSpec((tt, hk), lambda i: (i, 0))],
        out_shape=[jax.ShapeDtypeStruct((n_tok, hk), jnp.int32),
                   jax.ShapeDtypeStruct((n_tok, hk), F32)],
        scratch_shapes=[pltpu.VMEM((hk, tt), jnp.int32), pltpu.VMEM((hk, tt), F32),
                        pltpu.VMEM((2 * PEER_HEADS, N_KEYS, tt), F32), pltpu.VMEM((2, 4 * TOPK, tt), F32)],
        compiler_params=pltpu.CompilerParams(
            dimension_semantics=("parallel",), vmem_limit_bytes=VMEM_LIMIT),
        name="peer_route",
    )(xt, wk)


TOK_GROUP = 128


def _pack_kernel(w_ref, o_ref):
    n = w_ref.shape[0]
    for s in range(ROW_WORDS):
        lo = w_ref[:, 2 * s * LANES:(2 * s + 1) * LANES].astype(BF16).astype(F32)
        hi = w_ref[:, (2 * s + 1) * LANES:(2 * s + 2) * LANES].astype(BF16).astype(F32)
        word = pltpu.bitcast(hi, jnp.uint32) | (pltpu.bitcast(lo, jnp.uint32) >> 16)
        o_ref[pl.ds(s, n, stride=ROW_WORDS), :] = word


def _pack_rows(w, rows):
    n, d = w.shape
    assert d == 2 * ROW_WORDS * LANES and n % rows == 0
    return pl.pallas_call(
        _pack_kernel,
        grid=(n // rows,),
        in_specs=[pl.BlockSpec((rows, d), lambda i: (i, 0))],
        out_specs=pl.BlockSpec((rows * ROW_WORDS, LANES), lambda i: (i, 0)),
        out_shape=jax.ShapeDtypeStruct((n * ROW_WORDS, LANES), jnp.uint32),
        compiler_params=pltpu.CompilerParams(
            dimension_semantics=("parallel",), vmem_limit_bytes=VMEM_LIMIT),
        name="pack_rows",
    )(w)


def _gather_rows(idx_ref, t, tab_ref, hk):
    words = [tab_ref[pl.ds(pl.multiple_of(idx_ref[t, k], ROW_WORDS), ROW_WORDS), :] for k in range(hk)]
    return pltpu.bitcast(jnp.concatenate(words, axis=0), BF16)


def _diag_mask(hk):
    r = lax.broadcasted_iota(jnp.int32, (SUBLANES, hk * SUBLANES), 0)
    c = lax.broadcasted_iota(jnp.int32, (SUBLANES, hk * SUBLANES), 1)
    return (c & (SUBLANES - 1)) == r


def _idx_copy(idx_hbm, row0, buf, sem):
    return pltpu.make_async_copy(idx_hbm.at[pl.ds(row0, TOK_GROUP)], buf, sem)


def _for_each_group(idx_hbm, tb, idx_a, idx_b, sems, process):
    step = pl.program_id(0)
    base = step * tb
    last = idx_hbm.shape[0] - TOK_GROUP

    @pl.when(step == 0)
    def _():
        _idx_copy(idx_hbm, 0, idx_a, sems.at[0]).start()

    def pair(i, carry):
        t0 = pl.multiple_of(i * (2 * TOK_GROUP), 2 * TOK_GROUP)
        _idx_copy(idx_hbm, base + t0 + TOK_GROUP, idx_b, sems.at[1]).start()
        _idx_copy(idx_hbm, 0, idx_a, sems.at[0]).wait()
        process(idx_a, t0, 0)
        _idx_copy(idx_hbm, jnp.minimum(base + t0 + 2 * TOK_GROUP, last), idx_a, sems.at[0]).start()
        _idx_copy(idx_hbm, 0, idx_b, sems.at[1]).wait()
        process(idx_b, t0 + TOK_GROUP, 1)
        return carry

    lax.fori_loop(0, tb // (2 * TOK_GROUP), pair, 0)

    @pl.when(step == pl.num_programs(0) - 1)
    def _():
        _idx_copy(idx_hbm, 0, idx_a, sems.at[0]).wait()


def _idx_scratch(hk):
    return [pltpu.SMEM((TOK_GROUP, hk), jnp.int32), pltpu.SMEM((TOK_GROUP, hk), jnp.int32),
            pltpu.SemaphoreType.DMA((2,))]


def _down_kernel(idx_hbm, x_ref, gate_ref, fold_ref, tab_ref, c_ref, idx_a, idx_b, sems, xt_ref,
                 part_a, part_b):
    hk = gate_ref.shape[1]
    tb = x_ref.shape[0]
    mask = _diag_mask(hk)
    parts = (part_a, part_b)

    def finish(part_ref, t0):
        z = _dot(part_ref[...].astype(BF16), fold_ref[...])
        act = jnp.concatenate(
            [jnp.sum(z[j * SUBLANES:(j + 1) * SUBLANES], axis=0, keepdims=True)
             for j in range(TOK_GROUP)], axis=0)
        c_ref[pl.ds(t0, TOK_GROUP), :] = gate_ref[pl.ds(t0, TOK_GROUP), :] * _gelu(act)

    def process(idx_ref, t0, slot):
        finish(parts[1 - slot], pl.multiple_of(jnp.maximum(t0 - TOK_GROUP, 0), TOK_GROUP))
        xs = x_ref[pl.ds(t0, TOK_GROUP), :]
        for r in range(SUBLANES):
            xt_ref[pl.ds(r, TOK_GROUP, stride=SUBLANES), :] = xs[:, r * LANES:(r + 1) * LANES]
        for j in range(TOK_GROUP):
            rows = _gather_rows(idx_ref, j, tab_ref, hk)
            xj = xt_ref[j * SUBLANES:(j + 1) * SUBLANES, :].astype(BF16)
            s = _dot_nt(xj, rows)
            parts[slot][j * SUBLANES:(j + 1) * SUBLANES, :] = jnp.where(mask, s, 0.0)

    part_b[...] = jnp.zeros(part_b.shape, F32)
    _for_each_group(idx_hbm, tb, idx_a, idx_b, sems, process)
    finish(part_b, tb - TOK_GROUP)


def _peer_down(idx, xt, gate, tab, tb):
    n_tok, hk = idx.shape
    assert tb % (2 * TOK_GROUP) == 0 and xt.shape[1] == SUBLANES * LANES
    fold = (jnp.arange(hk * SUBLANES)[:, None] // SUBLANES == jnp.arange(hk)[None, :]).astype(BF16)
    return pl.pallas_call(
        _down_kernel,
        grid=(n_tok // tb,),
        in_specs=[
            pl.BlockSpec(memory_space=pl.ANY),
            pl.BlockSpec((tb, SUBLANES * LANES), lambda i: (i, 0)),
            pl.BlockSpec((tb, hk), lambda i: (i, 0)),
            pl.BlockSpec(fold.shape, lambda i: (0, 0)),
            pl.BlockSpec(memory_space=pltpu.VMEM),
        ],
        out_specs=pl.BlockSpec((tb, hk), lambda i: (i, 0)),
        out_shape=jax.ShapeDtypeStruct((n_tok, hk), F32),
        scratch_shapes=_idx_scratch(hk) + [pltpu.VMEM((TOK_GROUP * SUBLANES, LANES), F32)]
        + [pltpu.VMEM((TOK_GROUP * SUBLANES, hk * SUBLANES), F32) for _ in range(2)],
        compiler_params=pltpu.CompilerParams(
            dimension_semantics=("arbitrary",), vmem_limit_bytes=VMEM_LIMIT),
        name="peer_down",
    )(idx, xt, gate, fold, tab)


def _up_kernel(idx_hbm, c_ref, x_ref, spread_ref, tab_ref, g_ref, b_ref, o_ref, idx_a, idx_b, sems,
               ht_a, ht_b, *, alpha):
    hk = c_ref.shape[1]
    tb = c_ref.shape[0]
    mask = _diag_mask(hk)
    hts = (ht_a, ht_b)

    def finish(ht_ref, t0):
        h = jnp.concatenate([ht_ref[pl.ds(r, TOK_GROUP, stride=SUBLANES), :] for r in range(SUBLANES)],
                            axis=-1)
        o_ref[pl.ds(t0, TOK_GROUP), :] = _ln(alpha * x_ref[pl.ds(t0, TOK_GROUP), :] + h,
                                              g_ref[...], b_ref[...])

    def process(idx_ref, t0, slot):
        finish(hts[1 - slot], pl.multiple_of(jnp.maximum(t0 - TOK_GROUP, 0), TOK_GROUP))
        cexp = _dot(c_ref[pl.ds(t0, TOK_GROUP), :].astype(BF16), spread_ref[...])
        for j in range(TOK_GROUP):
            rows = _gather_rows(idx_ref, j, tab_ref, hk)
            cj = jnp.where(mask, jnp.broadcast_to(cexp[j:j + 1, :], mask.shape), 0.0)
            hts[slot][j * SUBLANES:(j + 1) * SUBLANES, :] = _dot(cj.astype(BF16), rows)

    ht_b[...] = jnp.zeros(ht_b.shape, F32)
    _for_each_group(idx_hbm, tb, idx_a, idx_b, sems, process)
    finish(ht_b, tb - TOK_GROUP)


def _peer_up(idx, c, xt, tab, ln_g, ln_b, alpha, tb):
    n_tok, hk = idx.shape
    d = xt.shape[1]
    assert tb % (2 * TOK_GROUP) == 0 and d == SUBLANES * LANES
    spread = (jnp.arange(hk)[:, None] == jnp.arange(hk * SUBLANES)[None, :] // SUBLANES).astype(BF16)
    return pl.pallas_call(
        functools.partial(_up_kernel, alpha=alpha),
        grid=(n_tok // tb,),
        in_specs=[
            pl.BlockSpec(memory_space=pl.ANY),
            pl.BlockSpec((tb, hk), lambda i: (i, 0)),
            pl.BlockSpec((tb, d), lambda i: (i, 0)),
            pl.BlockSpec(spread.shape, lambda i: (0, 0)),
            pl.BlockSpec(memory_space=pltpu.VMEM),
            pl.BlockSpec((1, d), lambda i: (0, 0)),
            pl.BlockSpec((1, d), lambda i: (0, 0)),
        ],
        out_specs=pl.BlockSpec((tb, d), lambda i: (i, 0)),
        out_shape=jax.ShapeDtypeStruct((n_tok, d), F32),
        scratch_shapes=_idx_scratch(hk)
        + [pltpu.VMEM((TOK_GROUP * SUBLANES, LANES), F32) for _ in range(2)],
        compiler_params=pltpu.CompilerParams(
            dimension_semantics=("arbitrary",), vmem_limit_bytes=VMEM_LIMIT),
        name="peer_up",
    )(idx, c, xt, spread, tab, ln_g.reshape(1, -1), ln_b.reshape(1, -1))


def _tile(n, want):
    return want if n % want == 0 else n


def kernel(x, mem, w_in, b_in, sgu_g, sgu_b, w_spatial, b_spatial, conv_w, w_mix_out, ln1_g, ln1_b,
           w_xq, w_xkv, w_xo, ln2_g, ln2_b, w_pq, sub_keys, w_down, w_up, ln3_g, ln3_b):
    depth = w_in.shape[0]
    bsz, seq, d = x.shape
    assert d == SUBLANES * LANES and seq % CHUNK == 0
    alpha = np.float32((2 * depth) ** 0.25)
    n_tok = bsz * seq
    ts = _tile(seq, 512)
    for l in range(depth):
        x = _mixer(x, w_in[l], b_in[l], sgu_g[l], sgu_b[l], w_spatial[l], b_spatial[l], conv_w[l],
                   w_mix_out[l], ln1_g[l], ln1_b[l], alpha, ts)
        kv = _kv_proj(mem, w_xkv[l])
        x = _xattn(x, kv, w_xq[l], w_xo[l], ln2_g[l], ln2_b[l], alpha, ts)
        xt = x.reshape(n_tok, d)
        idx, gate = _peer_route(xt, w_pq[l], sub_keys[l], _tile(n_tok, 1024))
        tb = _tile(n_tok, 1024)
        pack = _tile(w_down.shape[1], 512)
        c = _peer_down(idx, xt, gate, _pack_rows(w_down[l], pack), tb)
        x = _peer_up(idx, c, xt, _pack_rows(w_up[l], pack), ln3_g[l], ln3_b[l], alpha, tb)
        x = x.reshape(bsz, seq, d)
    return x
```

```python
import functools

import jax
import jax.numpy as jnp
import numpy as np
from jax import lax
from jax.experimental import pallas as pl
from jax.experimental.pallas import tpu as pltpu

LN_EPS = 1e-5
LANES = 128
SUBLANES = 8
A_HEADS = 4
CHUNK = 128
X_HEADS = 4
PEER_HEADS = 8
N_KEYS = 128
TOPK = 16
ROW_WORDS = 4
VMEM_LIMIT = 56 * 1024 * 1024

F32 = jnp.float32
BF16 = jnp.bfloat16


def _gelu(x):
    return 0.5 * x * (1.0 + lax.erf(x * np.float32(2.0 ** -0.5)))


def _ln(v, g, b):
    mu = jnp.mean(v, axis=-1, keepdims=True)
    d = v - mu
    var = jnp.mean(d * d, axis=-1, keepdims=True)
    return d * lax.rsqrt(var + LN_EPS) * g + b


def _dot(a, b):
    return jnp.dot(a, b, preferred_element_type=F32)


def _dot_nt(a, b):
    return lax.dot_general(a, b, (((1,), (1,)), ((), ())), preferred_element_type=F32)


def _mixer_kernel(x_ref, win_ref, bin_ref, sg_ref, sb_ref, ws_ref, bsp_ref, cw_ref, wout_ref,
                  lg_ref, lb_ref, o_ref, zc_ref, *, alpha, ts, aw):
    @pl.when(pl.program_id(1) == 0)
    def _():
        zc_ref[0:SUBLANES, :] = jnp.zeros((SUBLANES, zc_ref.shape[1]), F32)

    x = x_ref[0]
    xb = x.astype(BF16)

    def zcol(lo, hi):
        return _dot(xb, win_ref[:, lo:hi]) + bin_ref[:, lo:hi]

    hd = aw // A_HEADS
    u = _gelu(zcol(0, aw))
    v = _gelu(zcol(aw, 2 * aw))
    mixed = []
    for h in range(A_HEADS):
        sl = slice(h * hd, (h + 1) * hd)
        vh = _ln(v[:, sl], sg_ref[:, sl], sb_ref[:, sl]).astype(BF16)
        mixed.append(jnp.concatenate(
            [_dot(ws_ref[h], vh[c * CHUNK:(c + 1) * CHUNK]) + bsp_ref[:, sl]
             for c in range(ts // CHUNK)], axis=0))
    ya = u * jnp.concatenate(mixed, axis=-1)

    bw = zc_ref.shape[1]
    hb = zcol(2 * aw, 2 * aw + bw)
    gb = zcol(2 * aw + bw, 2 * aw + 2 * bw)
    gc = zcol(2 * aw + 2 * bw, 2 * aw + 3 * bw)
    zc_ref[SUBLANES:SUBLANES + ts, :] = gc * hb
    conv = (cw_ref[0:1, :] * zc_ref[SUBLANES - 2:SUBLANES - 2 + ts, :]
            + cw_ref[1:2, :] * zc_ref[SUBLANES - 1:SUBLANES - 1 + ts, :]
            + cw_ref[2:3, :] * zc_ref[SUBLANES:SUBLANES + ts, :])
    yb = gb * conv
    zc_ref[0:SUBLANES, :] = zc_ref[ts:ts + SUBLANES, :]

    h = _dot(ya.astype(BF16), wout_ref[0:aw, :]) + _dot(yb.astype(BF16), wout_ref[aw:, :])
    o_ref[0] = _ln(alpha * x + h, lg_ref[...], lb_ref[...])


def _mixer(x, w_in, b_in, sgu_g, sgu_b, w_spatial, b_spatial, conv_w, w_out, ln_g, ln_b, alpha, ts):
    bsz, seq, d = x.shape
    aw = A_HEADS * sgu_g.shape[-1]
    bw = conv_w.shape[-1]
    causal = jnp.tril(jnp.ones((CHUNK, CHUNK), dtype=bool))
    ws = jnp.where(causal[None], w_spatial, 0.0).astype(BF16)
    bsp = jnp.repeat(b_spatial.T, aw // A_HEADS, axis=1)
    const = lambda *shape: pl.BlockSpec(shape, lambda b, i: (0,) * len(shape))
    return pl.pallas_call(
        functools.partial(_mixer_kernel, alpha=alpha, ts=ts, aw=aw),
        grid=(bsz, seq // ts),
        in_specs=[
            pl.BlockSpec((1, ts, d), lambda b, i: (b, i, 0)),
            const(d, w_in.shape[1]), const(1, w_in.shape[1]), const(1, aw), const(1, aw),
            const(A_HEADS, CHUNK, CHUNK), const(CHUNK, aw), const(conv_w.shape[0], bw),
            const(aw + bw, d), const(1, d), const(1, d),
        ],
        out_specs=pl.BlockSpec((1, ts, d), lambda b, i: (b, i, 0)),
        out_shape=jax.ShapeDtypeStruct(x.shape, F32),
        scratch_shapes=[pltpu.VMEM((ts + 2 * SUBLANES, bw), F32)],
        compiler_params=pltpu.CompilerParams(
            dimension_semantics=("parallel", "arbitrary"), vmem_limit_bytes=VMEM_LIMIT),
        name="mixer",
    )(x, w_in.astype(BF16), b_in.reshape(1, -1), sgu_g.reshape(1, -1), sgu_b.reshape(1, -1),
      ws, bsp, conv_w.reshape(conv_w.shape[0], bw), w_out.astype(BF16),
      ln_g.reshape(1, -1), ln_b.reshape(1, -1))


def _fold_mem_kernel(m_ref, wkv_ref, wq_ref, wo_ref, qk_ref, vo_ref):
    d = wq_ref.shape[0]
    m = m_ref.shape[1]
    dk = d // X_HEADS
    kv = _dot(m_ref[0].astype(BF16), wkv_ref[...]).astype(BF16)
    for h in range(X_HEADS):
        sl = slice(h * dk, (h + 1) * dk)
        qk_ref[0, :, h * m:(h + 1) * m] = _dot_nt(wq_ref[:, sl], kv[:, sl]).astype(BF16)
        vo_ref[0, h * m:(h + 1) * m, :] = _dot(kv[:, d + h * dk:d + (h + 1) * dk], wo_ref[sl, :]).astype(BF16)


def _fold_mem(mem, w_kv, w_q, w_o):
    bsz, m, d = mem.shape
    return pl.pallas_call(
        _fold_mem_kernel,
        grid=(bsz,),
        in_specs=[pl.BlockSpec((1, m, d), lambda b: (b, 0, 0)),
                  pl.BlockSpec((d, 2 * d), lambda b: (0, 0)),
                  pl.BlockSpec((d, d), lambda b: (0, 0)),
                  pl.BlockSpec((d, d), lambda b: (0, 0))],
        out_specs=[pl.BlockSpec((1, d, X_HEADS * m), lambda b: (b, 0, 0)),
                   pl.BlockSpec((1, X_HEADS * m, d), lambda b: (b, 0, 0))],
        out_shape=[jax.ShapeDtypeStruct((bsz, d, X_HEADS * m), BF16),
                   jax.ShapeDtypeStruct((bsz, X_HEADS * m, d), BF16)],
        compiler_params=pltpu.CompilerParams(
            dimension_semantics=("parallel",), vmem_limit_bytes=VMEM_LIMIT),
        name="fold_mem",
    )(mem, w_kv.astype(BF16), w_q.astype(BF16), w_o.astype(BF16))


def _xattn_kernel(x_ref, qk_ref, vo_ref, lg_ref, lb_ref, o_ref, *, alpha, scale):
    x = x_ref[0]
    m = qk_ref.shape[2] // X_HEADS
    s = _dot(x.astype(BF16), qk_ref[0]) * scale
    ps = []
    for h in range(X_HEADS):
        sh = s[:, h * m:(h + 1) * m]
        e = jnp.exp(sh - jnp.max(sh, axis=-1, keepdims=True))
        ps.append((e / jnp.sum(e, axis=-1, keepdims=True)).astype(BF16))
    o_ref[0] = _ln(alpha * x + _dot(jnp.concatenate(ps, axis=-1), vo_ref[0]), lg_ref[...], lb_ref[...])


def _xattn(x, qk, vo, ln_g, ln_b, alpha, ts):
    bsz, seq, d = x.shape
    hm = qk.shape[2]
    scale = np.float32((d // X_HEADS) ** -0.5)
    return pl.pallas_call(
        functools.partial(_xattn_kernel, alpha=alpha, scale=scale),
        grid=(bsz, seq // ts),
        in_specs=[
            pl.BlockSpec((1, ts, d), lambda b, i: (b, i, 0)),
            pl.BlockSpec((1, d, hm), lambda b, i: (b, 0, 0)),
            pl.BlockSpec((1, hm, d), lambda b, i: (b, 0, 0)),
            pl.BlockSpec((1, d), lambda b, i: (0, 0)),
            pl.BlockSpec((1, d), lambda b, i: (0, 0)),
        ],
        out_specs=pl.BlockSpec((1, ts, d), lambda b, i: (b, i, 0)),
        out_shape=jax.ShapeDtypeStruct(x.shape, F32),
        compiler_params=pltpu.CompilerParams(
            dimension_semantics=("parallel", "parallel"), vmem_limit_bytes=VMEM_LIMIT),
        name="xattn",
    )(x, qk, vo, ln_g.reshape(1, -1), ln_b.reshape(1, -1))


def _top_pairs(va, vb, ea, eb, rounds):
    assert rounds == 2 * SUBLANES
    lanes = va.shape[1]
    sub = lax.broadcasted_iota(jnp.int32, (SUBLANES, lanes), 0).astype(F32)
    big = np.float32(2 ** 24)
    v, e = [], []
    for i in range(rounds):
        col = va[i:i + 1] + vb[0:SUBLANES]
        live = rounds // (i + 1)
        v.append(col if live >= SUBLANES else jnp.where(sub < np.float32(live), col, -jnp.inf))
        e.append(ea[i:i + 1] + eb[0:SUBLANES])
    v_hi = va[0:1] + vb[SUBLANES:]
    e_hi = ea[0:1] + eb[SUBLANES:]
    code = sub
    code_hi = sub + np.float32(SUBLANES)
    tops, picks = [], []
    for r in range(rounds):
        m = jnp.max(jnp.maximum(v[0], v_hi), axis=0, keepdims=True)
        first = jnp.min(jnp.minimum(jnp.where(v[0] == m, code, big), jnp.where(v_hi == m, code_hi, big)),
                        axis=0, keepdims=True)
        hit = code == first
        hit_hi = code_hi == first
        tops.append(m)
        picks.append(jnp.max(jnp.maximum(jnp.where(hit, e[0], -1.0), jnp.where(hit_hi, e_hi, -1.0)),
                             axis=0, keepdims=True))
        v_hi = jnp.where(hit_hi, -jnp.inf, v_hi)
        code = jnp.where(hit, code + np.float32(rounds), code)
        for i in range(rounds - 1 - r):
            v[i] = jnp.where(hit, v[i + 1], v[i])
            e[i] = jnp.where(hit, e[i + 1], e[i])
    return jnp.concatenate(tops, axis=0), jnp.concatenate(picks, axis=0)


def _merge_sort_network(n):
    def merge(lo, hi, r):
        step = 2 * r
        if step < hi - lo:
            yield from merge(lo, hi, step)
            yield from merge(lo + r, hi, step)
            yield from ((i, i + r) for i in range(lo + r, hi - r, step))
        else:
            yield lo, lo + r

    def sort(lo, hi):
        if hi > lo:
            mid = lo + (hi - lo) // 2
            yield from sort(lo, mid)
            yield from sort(mid + 1, hi)
            yield from merge(lo, hi, 1)

    assert n & (n - 1) == 0
    return list(sort(0, n - 1))


def _top_rows(vals, rounds):
    n, lanes = vals.shape
    nt = n // SUBLANES
    assert rounds <= nt
    sub = lax.broadcasted_iota(jnp.int32, (SUBLANES, lanes), 0).astype(F32)
    v = [vals[k * SUBLANES:(k + 1) * SUBLANES] for k in range(nt)]
    idx = [sub + np.float32(SUBLANES * k) for k in range(nt)]
    for a, b in _merge_sort_network(nt):
        swap = (v[b] > v[a]) | ((v[b] == v[a]) & (idx[b] < idx[a]))
        v[a], v[b] = jnp.maximum(v[a], v[b]), jnp.minimum(v[a], v[b])
        idx[a], idx[b] = jnp.where(swap, idx[b], idx[a]), jnp.where(swap, idx[a], idx[b])
    tops, picks = [], []
    for r in range(rounds):
        m = jnp.max(v[0], axis=0, keepdims=True)
        first = jnp.min(jnp.where(v[0] == m, idx[0], np.float32(2 ** 24)), axis=0, keepdims=True)
        tops.append(m)
        picks.append(first)
        hit = idx[0] == first
        for k in range(rounds - 1 - r):
            v[k] = jnp.where(hit, v[k + 1], v[k])
            idx[k] = jnp.where(hit, idx[k + 1], idx[k])
    return jnp.concatenate(tops, axis=0), jnp.concatenate(picks, axis=0)


def _fold_keys_kernel(w_ref, k_ref, o_ref):
    o_ref[...] = lax.dot_general(k_ref[0], w_ref[...], (((1,), (1,)), ((), ())),
                                 preferred_element_type=F32, precision=lax.Precision.HIGHEST).astype(BF16)


def _fold_keys(w_query, sub_keys):
    d = w_query.shape[0]
    half_key = sub_keys.shape[-1]
    keys = sub_keys.reshape(-1, N_KEYS, half_key)
    return pl.pallas_call(
        _fold_keys_kernel,
        grid=(keys.shape[0],),
        in_specs=[pl.BlockSpec((d, half_key), lambda i: (0, i)),
                  pl.BlockSpec((1, N_KEYS, half_key), lambda i: (i, 0, 0))],
        out_specs=pl.BlockSpec((N_KEYS, d), lambda i: (i, 0)),
        out_shape=jax.ShapeDtypeStruct((keys.shape[0] * N_KEYS, d), BF16),
        compiler_params=pltpu.CompilerParams(
            dimension_semantics=("parallel",), vmem_limit_bytes=VMEM_LIMIT),
        name="fold_keys",
    )(w_query, keys)


def _route_kernel(x_ref, wk_ref, idx_ref, gate_ref, idx_t, gate_t, sc_ref, top_ref):
    sc = _dot_nt(wk_ref[...], x_ref[...].astype(BF16))
    for i in range(2 * PEER_HEADS):
        sc_ref[i] = sc[i * N_KEYS:(i + 1) * N_KEYS]
    top_ref[1] = jnp.zeros(top_ref.shape[1:], F32)

    def sub_top(h, slot):
        va, ia = _top_rows(sc_ref[2 * h], TOPK)
        vb, ib = _top_rows(sc_ref[2 * h + 1], TOPK)
        top_ref[slot] = jnp.concatenate([va, vb, ia, ib], axis=0)

    def pair_top(h, slot):
        t = top_ref[slot]
        va, vb, ia, ib = (t[i * TOPK:(i + 1) * TOPK] for i in range(4))
        top, expert = _top_pairs(va, vb, ia * np.float32(N_KEYS), ib, TOPK)
        e = jnp.exp(top - top[0:1])
        r0 = pl.multiple_of(h * TOPK, TOPK)
        gate_t[pl.ds(r0, TOPK), :] = e / jnp.sum(e, axis=0, keepdims=True)
        idx_t[pl.ds(r0, TOPK), :] = expert.astype(jnp.int32) * ROW_WORDS

    def two_heads(p, carry):
        h = 2 * p
        sub_top(h, 0)
        pair_top(jnp.maximum(h - 1, 0), 1)
        sub_top(h + 1, 1)
        pair_top(h, 0)
        return carry

    lax.fori_loop(0, PEER_HEADS // 2, two_heads, 0)
    pair_top(PEER_HEADS - 1, 1)
    idx_ref[...] = idx_t[...].T
    gate_ref[...] = gate_t[...].T


def _peer_route(xt, w_query, sub_keys, tt):
    n_tok, d = xt.shape
    wk = _fold_keys(w_query, sub_keys)
    hk = PEER_HEADS * TOPK
    return pl.pallas_call(
        _route_kernel,
        grid=(n_tok // tt,),
        in_specs=[
            pl.BlockSpec((tt, d), lambda i: (i, 0)),
            pl.BlockSpec(wk.shape, lambda i: (0, 0)),
        ],
        out_specs=[pl.BlockSpec((tt, hk), lambda i: (i, 0)),
                   pl.BlockSpec((tt, hk), lambda i: (i, 0))],
        out_shape=[jax.ShapeDtypeStruct((n_tok, hk), jnp.int32),
                   jax.ShapeDtypeStruct((n_tok, hk), F32)],
        scratch_shapes=[pltpu.VMEM((hk, tt), jnp.int32), pltpu.VMEM((hk, tt), F32),
                        pltpu.VMEM((2 * PEER_HEADS, N_KEYS, tt), F32), pltpu.VMEM((2, 4 * TOPK, tt), F32)],
        compiler_params=pltpu.CompilerParams(
            dimension_semantics=("parallel",), vmem_limit_bytes=VMEM_LIMIT),
        name="peer_route",
    )(xt, wk)


TOK_GROUP = 128


def _pack_kernel(w_ref, o_ref):
    n = w_ref.shape[0]
    for s in range(ROW_WORDS):
        lo = w_ref[:, 2 * s * LANES:(2 * s + 1) * LANES].astype(BF16).astype(F32)
        hi = w_ref[:, (2 * s + 1) * LANES:(2 * s + 2) * LANES].astype(BF16).astype(F32)
        word = pltpu.bitcast(hi, jnp.uint32) | (pltpu.bitcast(lo, jnp.uint32) >> 16)
        o_ref[pl.ds(s, n, stride=ROW_WORDS), :] = word


def _pack_rows(w, rows):
    n, d = w.shape
    assert d == 2 * ROW_WORDS * LANES and n % rows == 0
    return pl.pallas_call(
        _pack_kernel,
        grid=(n // rows,),
        in_specs=[pl.BlockSpec((rows, d), lambda i: (i, 0))],
        out_specs=pl.BlockSpec((rows * ROW_WORDS, LANES), lambda i: (i, 0)),
        out_shape=jax.ShapeDtypeStruct((n * ROW_WORDS, LANES), jnp.uint32),
        compiler_params=pltpu.CompilerParams(
            dimension_semantics=("parallel",), vmem_limit_bytes=VMEM_LIMIT),
        name="pack_rows",
    )(w)


def _gather_rows(idx_ref, t, tab_ref, hk):
    words = [tab_ref[pl.ds(pl.multiple_of(idx_ref[t, k], ROW_WORDS), ROW_WORDS), :] for k in range(hk)]
    return pltpu.bitcast(jnp.concatenate(words, axis=0), BF16)


def _diag_mask(hk):
    r = lax.broadcasted_iota(jnp.int32, (SUBLANES, hk * SUBLANES), 0)
    c = lax.broadcasted_iota(jnp.int32, (SUBLANES, hk * SUBLANES), 1)
    return (c & (SUBLANES - 1)) == r


def _idx_copy(idx_hbm, row0, buf, sem):
    return pltpu.make_async_copy(idx_hbm.at[pl.ds(row0, TOK_GROUP)], buf, sem)


def _for_each_group(idx_hbm, tb, idx_a, idx_b, sems, process):
    step = pl.program_id(0)
    base = step * tb
    last = idx_hbm.shape[0] - TOK_GROUP

    @pl.when(step == 0)
    def _():
        _idx_copy(idx_hbm, 0, idx_a, sems.at[0]).start()

    def pair(i, carry):
        t0 = pl.multiple_of(i * (2 * TOK_GROUP), 2 * TOK_GROUP)
        _idx_copy(idx_hbm, base + t0 + TOK_GROUP, idx_b, sems.at[1]).start()
        _idx_copy(idx_hbm, 0, idx_a, sems.at[0]).wait()
        process(idx_a, t0, 0)
        _idx_copy(idx_hbm, jnp.minimum(base + t0 + 2 * TOK_GROUP, last), idx_a, sems.at[0]).start()
        _idx_copy(idx_hbm, 0, idx_b, sems.at[1]).wait()
        process(idx_b, t0 + TOK_GROUP, 1)
        return carry

    lax.fori_loop(0, tb // (2 * TOK_GROUP), pair, 0)

    @pl.when(step == pl.num_programs(0) - 1)
    def _():
        _idx_copy(idx_hbm, 0, idx_a, sems.at[0]).wait()


def _idx_scratch(hk):
    return [pltpu.SMEM((TOK_GROUP, hk), jnp.int32), pltpu.SMEM((TOK_GROUP, hk), jnp.int32),
            pltpu.SemaphoreType.DMA((2,))]


def _down_kernel(idx_hbm, x_ref, gate_ref, fold_ref, tab_ref, c_ref, idx_a, idx_b, sems, xt_ref,
                 part_a, part_b):
    hk = gate_ref.shape[1]
    tb = x_ref.shape[0]
    mask = _diag_mask(hk)
    parts = (part_a, part_b)

    def finish(part_ref, t0):
        z = _dot(part_ref[...].astype(BF16), fold_ref[...])
        act = jnp.concatenate(
            [jnp.sum(z[j * SUBLANES:(j + 1) * SUBLANES], axis=0, keepdims=True)
             for j in range(TOK_GROUP)], axis=0)
        c_ref[pl.ds(t0, TOK_GROUP), :] = gate_ref[pl.ds(t0, TOK_GROUP), :] * _gelu(act)

    def process(idx_ref, t0, slot):
        finish(parts[1 - slot], pl.multiple_of(jnp.maximum(t0 - TOK_GROUP, 0), TOK_GROUP))
        xs = x_ref[pl.ds(t0, TOK_GROUP), :]
        for r in range(SUBLANES):
            xt_ref[pl.ds(r, TOK_GROUP, stride=SUBLANES), :] = xs[:, r * LANES:(r + 1) * LANES]
        for j in range(TOK_GROUP):
            rows = _gather_rows(idx_ref, j, tab_ref, hk)
            xj = xt_ref[j * SUBLANES:(j + 1) * SUBLANES, :].astype(BF16)
            s = _dot_nt(xj, rows)
            parts[slot][j * SUBLANES:(j + 1) * SUBLANES, :] = jnp.where(mask, s, 0.0)

    part_b[...] = jnp.zeros(part_b.shape, F32)
    _for_each_group(idx_hbm, tb, idx_a, idx_b, sems, process)
    finish(part_b, tb - TOK_GROUP)


def _peer_down(idx, xt, gate, tab, tb):
    n_tok, hk = idx.shape
    assert tb % (2 * TOK_GROUP) == 0 and xt.shape[1] == SUBLANES * LANES
    fold = (jnp.arange(hk * SUBLANES)[:, None] // SUBLANES == jnp.arange(hk)[None, :]).astype(BF16)
    return pl.pallas_call(
        _down_kernel,
        grid=(n_tok // tb,),
        in_specs=[
            pl.BlockSpec(memory_space=pl.ANY),
            pl.BlockSpec((tb, SUBLANES * LANES), lambda i: (i, 0)),
            pl.BlockSpec((tb, hk), lambda i: (i, 0)),
            pl.BlockSpec(fold.shape, lambda i: (0, 0)),
            pl.BlockSpec(memory_space=pltpu.VMEM),
        ],
        out_specs=pl.BlockSpec((tb, hk), lambda i: (i, 0)),
        out_shape=jax.ShapeDtypeStruct((n_tok, hk), F32),
        scratch_shapes=_idx_scratch(hk) + [pltpu.VMEM((TOK_GROUP * SUBLANES, LANES), F32)]
        + [pltpu.VMEM((TOK_GROUP * SUBLANES, hk * SUBLANES), F32) for _ in range(2)],
        compiler_params=pltpu.CompilerParams(
            dimension_semantics=("arbitrary",), vmem_limit_bytes=VMEM_LIMIT),
        name="peer_down",
    )(idx, xt, gate, fold, tab)


def _up_kernel(idx_hbm, c_ref, x_ref, spread_ref, tab_ref, g_ref, b_ref, o_ref, idx_a, idx_b, sems,
               ht_a, ht_b, *, alpha):
    hk = c_ref.shape[1]
    tb = c_ref.shape[0]
    mask = _diag_mask(hk)
    hts = (ht_a, ht_b)

    def finish(ht_ref, t0):
        h = jnp.concatenate([ht_ref[pl.ds(r, TOK_GROUP, stride=SUBLANES), :] for r in range(SUBLANES)],
                            axis=-1)
        o_ref[pl.ds(t0, TOK_GROUP), :] = _ln(alpha * x_ref[pl.ds(t0, TOK_GROUP), :] + h,
                                              g_ref[...], b_ref[...])

    def process(idx_ref, t0, slot):
        finish(hts[1 - slot], pl.multiple_of(jnp.maximum(t0 - TOK_GROUP, 0), TOK_GROUP))
        cexp = _dot(c_ref[pl.ds(t0, TOK_GROUP), :].astype(BF16), spread_ref[...])
        for j in range(TOK_GROUP):
            rows = _gather_rows(idx_ref, j, tab_ref, hk)
            cj = jnp.where(mask, jnp.broadcast_to(cexp[j:j + 1, :], mask.shape), 0.0)
            hts[slot][j * SUBLANES:(j + 1) * SUBLANES, :] = _dot(cj.astype(BF16), rows)

    ht_b[...] = jnp.zeros(ht_b.shape, F32)
    _for_each_group(idx_hbm, tb, idx_a, idx_b, sems, process)
    finish(ht_b, tb - TOK_GROUP)


def _peer_up(idx, c, xt, tab, ln_g, ln_b, alpha, tb):
    n_tok, hk = idx.shape
    d = xt.shape[1]
    assert tb % (2 * TOK_GROUP) == 0 and d == SUBLANES * LANES
    spread = (jnp.arange(hk)[:, None] == jnp.arange(hk * SUBLANES)[None, :] // SUBLANES).astype(BF16)
    return pl.pallas_call(
        functools.partial(_up_kernel, alpha=alpha),
        grid=(n_tok // tb,),
        in_specs=[
            pl.BlockSpec(memory_space=pl.ANY),
            pl.BlockSpec((tb, hk), lambda i: (i, 0)),
            pl.BlockSpec((tb, d), lambda i: (i, 0)),
            pl.BlockSpec(spread.shape, lambda i: (0, 0)),
            pl.BlockSpec(memory_space=pltpu.VMEM),
            pl.BlockSpec((1, d), lambda i: (0, 0)),
            pl.BlockSpec((1, d), lambda i: (0, 0)),
        ],
        out_specs=pl.BlockSpec((tb, d), lambda i: (i, 0)),
        out_shape=jax.ShapeDtypeStruct((n_tok, d), F32),
        scratch_shapes=_idx_scratch(hk)
        + [pltpu.VMEM((TOK_GROUP * SUBLANES, LANES), F32) for _ in range(2)],
        compiler_params=pltpu.CompilerParams(
            dimension_semantics=("arbitrary",), vmem_limit_bytes=VMEM_LIMIT),
        name="peer_up",
    )(idx, c, xt, spread, tab, ln_g.reshape(1, -1), ln_b.reshape(1, -1))


def _tile(n, want):
    return want if n % want == 0 else n


def kernel(x, mem, w_in, b_in, sgu_g, sgu_b, w_spatial, b_spatial, conv_w, w_mix_out, ln1_g, ln1_b,
           w_xq, w_xkv, w_xo, ln2_g, ln2_b, w_pq, sub_keys, w_down, w_up, ln3_g, ln3_b):
    depth = w_in.shape[0]
    bsz, seq, d = x.shape
    assert d == SUBLANES * LANES and seq % CHUNK == 0
    alpha = np.float32((2 * depth) ** 0.25)
    n_tok = bsz * seq
    ts = _tile(seq, 512)
    for l in range(depth):
        x = _mixer(x, w_in[l], b_in[l], sgu_g[l], sgu_b[l], w_spatial[l], b_spatial[l], conv_w[l],
                   w_mix_out[l], ln1_g[l], ln1_b[l], alpha, ts)
        qk, vo = _fold_mem(mem, w_xkv[l], w_xq[l], w_xo[l])
        x = _xattn(x, qk, vo, ln2_g[l], ln2_b[l], alpha, ts)
        xt = x.reshape(n_tok, d)
        idx, gate = _peer_route(xt, w_pq[l], sub_keys[l], _tile(n_tok, 1024))
        tb = _tile(n_tok, 1024)
        pack = _tile(w_down.shape[1], 512)
        c = _peer_down(idx, xt, gate, _pack_rows(w_down[l], pack), tb)
        x = _peer_up(idx, c, xt, _pack_rows(w_up[l], pack), ln3_g[l], ln3_b[l], alpha, tb)
        x = x.reshape(bsz, seq, d)
    return x
```

```python
import functools

import jax
import jax.numpy as jnp
import numpy as np
from jax import lax
from jax.experimental import pallas as pl
from jax.experimental.pallas import tpu as pltpu

LN_EPS = 1e-5
LANES = 128
SUBLANES = 8
A_HEADS = 4
CHUNK = 128
X_HEADS = 4
PEER_HEADS = 8
N_KEYS = 128
TOPK = 16
ROW_WORDS = 4
VMEM_LIMIT = 56 * 1024 * 1024

F32 = jnp.float32
BF16 = jnp.bfloat16


def _gelu(x):
    return 0.5 * x * (1.0 + lax.erf(x * np.float32(2.0 ** -0.5)))


def _ln(v, g, b):
    mu = jnp.mean(v, axis=-1, keepdims=True)
    d = v - mu
    var = jnp.mean(d * d, axis=-1, keepdims=True)
    return d * lax.rsqrt(var + LN_EPS) * g + b


def _dot(a, b):
    return jnp.dot(a, b, preferred_element_type=F32)


def _dot_nt(a, b):
    return lax.dot_general(a, b, (((1,), (1,)), ((), ())), preferred_element_type=F32)


def _mixer_kernel(x_ref, win_ref, bin_ref, sg_ref, sb_ref, ws_ref, bsp_ref, cw_ref, wout_ref,
                  lg_ref, lb_ref, qk_ref, vo_ref, lg2_ref, lb2_ref, o_ref, zc_ref, *, alpha, ts, aw, scale):
    @pl.when(pl.program_id(1) == 0)
    def _():
        zc_ref[0:SUBLANES, :] = jnp.zeros((SUBLANES, zc_ref.shape[1]), F32)

    x = x_ref[0]
    xb = x.astype(BF16)

    def zcol(lo, hi):
        return _dot(xb, win_ref[:, lo:hi]) + bin_ref[:, lo:hi]

    hd = aw // A_HEADS
    u = _gelu(zcol(0, aw))
    v = _gelu(zcol(aw, 2 * aw))
    mixed = []
    for h in range(A_HEADS):
        sl = slice(h * hd, (h + 1) * hd)
        vh = _ln(v[:, sl], sg_ref[:, sl], sb_ref[:, sl]).astype(BF16)
        mixed.append(jnp.concatenate(
            [_dot(ws_ref[h], vh[c * CHUNK:(c + 1) * CHUNK]) + bsp_ref[:, sl]
             for c in range(ts // CHUNK)], axis=0))
    ya = u * jnp.concatenate(mixed, axis=-1)

    bw = zc_ref.shape[1]
    hb = zcol(2 * aw, 2 * aw + bw)
    gb = zcol(2 * aw + bw, 2 * aw + 2 * bw)
    gc = zcol(2 * aw + 2 * bw, 2 * aw + 3 * bw)
    zc_ref[SUBLANES:SUBLANES + ts, :] = gc * hb
    conv = (cw_ref[0:1, :] * zc_ref[SUBLANES - 2:SUBLANES - 2 + ts, :]
            + cw_ref[1:2, :] * zc_ref[SUBLANES - 1:SUBLANES - 1 + ts, :]
            + cw_ref[2:3, :] * zc_ref[SUBLANES:SUBLANES + ts, :])
    yb = gb * conv
    zc_ref[0:SUBLANES, :] = zc_ref[ts:ts + SUBLANES, :]

    h = _dot(ya.astype(BF16), wout_ref[0:aw, :]) + _dot(yb.astype(BF16), wout_ref[aw:, :])
    x1 = _ln(alpha * x + h, lg_ref[...], lb_ref[...])
    m = qk_ref.shape[2] // X_HEADS
    sc = _dot(x1.astype(BF16), qk_ref[0]) * scale
    ps = []
    for hh in range(X_HEADS):
        sh = sc[:, hh * m:(hh + 1) * m]
        e = jnp.exp(sh - jnp.max(sh, axis=-1, keepdims=True))
        ps.append((e / jnp.sum(e, axis=-1, keepdims=True)).astype(BF16))
    o_ref[0] = _ln(alpha * x1 + _dot(jnp.concatenate(ps, axis=-1), vo_ref[0]), lg2_ref[...], lb2_ref[...])


def _mixer(x, w_in, b_in, sgu_g, sgu_b, w_spatial, b_spatial, conv_w, w_out, ln_g, ln_b, qk, vo, ln2_g, ln2_b,
           alpha, ts):
    bsz, seq, d = x.shape
    aw = A_HEADS * sgu_g.shape[-1]
    bw = conv_w.shape[-1]
    causal = jnp.tril(jnp.ones((CHUNK, CHUNK), dtype=bool))
    ws = jnp.where(causal[None], w_spatial, 0.0).astype(BF16)
    bsp = jnp.repeat(b_spatial.T, aw // A_HEADS, axis=1)
    const = lambda *shape: pl.BlockSpec(shape, lambda b, i: (0,) * len(shape))
    return pl.pallas_call(
        functools.partial(_mixer_kernel, alpha=alpha, ts=ts, aw=aw, scale=np.float32((d // X_HEADS) ** -0.5)),
        grid=(bsz, seq // ts),
        in_specs=[
            pl.BlockSpec((1, ts, d), lambda b, i: (b, i, 0)),
            const(d, w_in.shape[1]), const(1, w_in.shape[1]), const(1, aw), const(1, aw),
            const(A_HEADS, CHUNK, CHUNK), const(CHUNK, aw), const(conv_w.shape[0], bw),
            const(aw + bw, d), const(1, d), const(1, d),
            pl.BlockSpec((1, d, qk.shape[2]), lambda b, i: (b, 0, 0)),
            pl.BlockSpec((1, qk.shape[2], d), lambda b, i: (b, 0, 0)),
            const(1, d), const(1, d),
        ],
        out_specs=pl.BlockSpec((1, ts, d), lambda b, i: (b, i, 0)),
        out_shape=jax.ShapeDtypeStruct(x.shape, F32),
        scratch_shapes=[pltpu.VMEM((ts + 2 * SUBLANES, bw), F32)],
        compiler_params=pltpu.CompilerParams(
            dimension_semantics=("parallel", "arbitrary"), vmem_limit_bytes=VMEM_LIMIT),
        name="mixer",
    )(x, w_in.astype(BF16), b_in.reshape(1, -1), sgu_g.reshape(1, -1), sgu_b.reshape(1, -1),
      ws, bsp, conv_w.reshape(conv_w.shape[0], bw), w_out.astype(BF16),
      ln_g.reshape(1, -1), ln_b.reshape(1, -1), qk, vo, ln2_g.reshape(1, -1), ln2_b.reshape(1, -1))


def _fold_mem_kernel(m_ref, wkv_ref, wq_ref, wo_ref, qk_ref, vo_ref):
    d = wq_ref.shape[0]
    m = m_ref.shape[1]
    dk = d // X_HEADS
    kv = _dot(m_ref[0].astype(BF16), wkv_ref[...]).astype(BF16)
    for h in range(X_HEADS):
        sl = slice(h * dk, (h + 1) * dk)
        qk_ref[0, :, h * m:(h + 1) * m] = _dot_nt(wq_ref[:, sl], kv[:, sl]).astype(BF16)
        vo_ref[0, h * m:(h + 1) * m, :] = _dot(kv[:, d + h * dk:d + (h + 1) * dk], wo_ref[sl, :]).astype(BF16)


def _fold_mem(mem, w_kv, w_q, w_o):
    bsz, m, d = mem.shape
    return pl.pallas_call(
        _fold_mem_kernel,
        grid=(bsz,),
        in_specs=[pl.BlockSpec((1, m, d), lambda b: (b, 0, 0)),
                  pl.BlockSpec((d, 2 * d), lambda b: (0, 0)),
                  pl.BlockSpec((d, d), lambda b: (0, 0)),
                  pl.BlockSpec((d, d), lambda b: (0, 0))],
        out_specs=[pl.BlockSpec((1, d, X_HEADS * m), lambda b: (b, 0, 0)),
                   pl.BlockSpec((1, X_HEADS * m, d), lambda b: (b, 0, 0))],
        out_shape=[jax.ShapeDtypeStruct((bsz, d, X_HEADS * m), BF16),
                   jax.ShapeDtypeStruct((bsz, X_HEADS * m, d), BF16)],
        compiler_params=pltpu.CompilerParams(
            dimension_semantics=("parallel",), vmem_limit_bytes=VMEM_LIMIT),
        name="fold_mem",
    )(mem, w_kv.astype(BF16), w_q.astype(BF16), w_o.astype(BF16))


def _top_pairs(va, vb, ea, eb, rounds):
    assert rounds == 2 * SUBLANES
    lanes = va.shape[1]
    sub = lax.broadcasted_iota(jnp.int32, (SUBLANES, lanes), 0).astype(F32)
    big = np.float32(2 ** 24)
    v, e = [], []
    for i in range(rounds):
        col = va[i:i + 1] + vb[0:SUBLANES]
        live = rounds // (i + 1)
        v.append(col if live >= SUBLANES else jnp.where(sub < np.float32(live), col, -jnp.inf))
        e.append(ea[i:i + 1] + eb[0:SUBLANES])
    v_hi = va[0:1] + vb[SUBLANES:]
    e_hi = ea[0:1] + eb[SUBLANES:]
    code = sub
    code_hi = sub + np.float32(SUBLANES)
    tops, picks = [], []
    for r in range(rounds):
        m = jnp.max(jnp.maximum(v[0], v_hi), axis=0, keepdims=True)
        first = jnp.min(jnp.minimum(jnp.where(v[0] == m, code, big), jnp.where(v_hi == m, code_hi, big)),
                        axis=0, keepdims=True)
        hit = code == first
        hit_hi = code_hi == first
        tops.append(m)
        picks.append(jnp.max(jnp.maximum(jnp.where(hit, e[0], -1.0), jnp.where(hit_hi, e_hi, -1.0)),
                             axis=0, keepdims=True))
        v_hi = jnp.where(hit_hi, -jnp.inf, v_hi)
        code = jnp.where(hit, code + np.float32(rounds), code)
        for i in range(rounds - 1 - r):
            v[i] = jnp.where(hit, v[i + 1], v[i])
            e[i] = jnp.where(hit, e[i + 1], e[i])
    return jnp.concatenate(tops, axis=0), jnp.concatenate(picks, axis=0)


def _merge_sort_network(n):
    def merge(lo, hi, r):
        step = 2 * r
        if step < hi - lo:
            yield from merge(lo, hi, step)
            yield from merge(lo + r, hi, step)
            yield from ((i, i + r) for i in range(lo + r, hi - r, step))
        else:
            yield lo, lo + r

    def sort(lo, hi):
        if hi > lo:
            mid = lo + (hi - lo) // 2
            yield from sort(lo, mid)
            yield from sort(mid + 1, hi)
            yield from merge(lo, hi, 1)

    assert n & (n - 1) == 0
    return list(sort(0, n - 1))


def _top_rows(vals, rounds):
    n, lanes = vals.shape
    nt = n // SUBLANES
    assert rounds <= nt
    sub = lax.broadcasted_iota(jnp.int32, (SUBLANES, lanes), 0).astype(F32)
    v = [vals[k * SUBLANES:(k + 1) * SUBLANES] for k in range(nt)]
    idx = [sub + np.float32(SUBLANES * k) for k in range(nt)]
    for a, b in _merge_sort_network(nt):
        swap = (v[b] > v[a]) | ((v[b] == v[a]) & (idx[b] < idx[a]))
        v[a], v[b] = jnp.maximum(v[a], v[b]), jnp.minimum(v[a], v[b])
        idx[a], idx[b] = jnp.where(swap, idx[b], idx[a]), jnp.where(swap, idx[a], idx[b])
    tops, picks = [], []
    for r in range(rounds):
        m = jnp.max(v[0], axis=0, keepdims=True)
        first = jnp.min(jnp.where(v[0] == m, idx[0], np.float32(2 ** 24)), axis=0, keepdims=True)
        tops.append(m)
        picks.append(first)
        hit = idx[0] == first
        for k in range(rounds - 1 - r):
            v[k] = jnp.where(hit, v[k + 1], v[k])
            idx[k] = jnp.where(hit, idx[k + 1], idx[k])
    return jnp.concatenate(tops, axis=0), jnp.concatenate(picks, axis=0)


def _fold_keys_kernel(w_ref, k_ref, o_ref):
    o_ref[...] = lax.dot_general(k_ref[0], w_ref[...], (((1,), (1,)), ((), ())),
                                 preferred_element_type=F32, precision=lax.Precision.HIGHEST).astype(BF16)


def _fold_keys(w_query, sub_keys):
    d = w_query.shape[0]
    half_key = sub_keys.shape[-1]
    keys = sub_keys.reshape(-1, N_KEYS, half_key)
    return pl.pallas_call(
        _fold_keys_kernel,
        grid=(keys.shape[0],),
        in_specs=[pl.BlockSpec((d, half_key), lambda i: (0, i)),
                  pl.BlockSpec((1, N_KEYS, half_key), lambda i: (i, 0, 0))],
        out_specs=pl.BlockSpec((N_KEYS, d), lambda i: (i, 0)),
        out_shape=jax.ShapeDtypeStruct((keys.shape[0] * N_KEYS, d), BF16),
        compiler_params=pltpu.CompilerParams(
            dimension_semantics=("parallel",), vmem_limit_bytes=VMEM_LIMIT),
        name="fold_keys",
    )(w_query, keys)


def _route_kernel(x_ref, wk_ref, idx_ref, gate_ref, idx_t, gate_t, sc_ref, top_ref):
    sc = _dot_nt(wk_ref[...], x_ref[...].astype(BF16))
    for i in range(2 * PEER_HEADS):
        sc_ref[i] = sc[i * N_KEYS:(i + 1) * N_KEYS]
    top_ref[1] = jnp.zeros(top_ref.shape[1:], F32)

    def sub_top(h, slot):
        va, ia = _top_rows(sc_ref[2 * h], TOPK)
        vb, ib = _top_rows(sc_ref[2 * h + 1], TOPK)
        top_ref[slot] = jnp.concatenate([va, vb, ia, ib], axis=0)

    def pair_top(h, slot):
        t = top_ref[slot]
        va, vb, ia, ib = (t[i * TOPK:(i + 1) * TOPK] for i in range(4))
        top, expert = _top_pairs(va, vb, ia * np.float32(N_KEYS), ib, TOPK)
        e = jnp.exp(top - top[0:1])
        r0 = pl.multiple_of(h * TOPK, TOPK)
        gate_t[pl.ds(r0, TOPK), :] = e / jnp.sum(e, axis=0, keepdims=True)
        idx_t[pl.ds(r0, TOPK), :] = expert.astype(jnp.int32) * ROW_WORDS

    def two_heads(p, carry):
        h = 2 * p
        sub_top(h, 0)
        pair_top(jnp.maximum(h - 1, 0), 1)
        sub_top(h + 1, 1)
        pair_top(h, 0)
        return carry

    lax.fori_loop(0, PEER_HEADS // 2, two_heads, 0)
    pair_top(PEER_HEADS - 1, 1)
    idx_ref[...] = idx_t[...].T
    gate_ref[...] = gate_t[...].T


def _peer_route(xt, w_query, sub_keys, tt):
    n_tok, d = xt.shape
    wk = _fold_keys(w_query, sub_keys)
    hk = PEER_HEADS * TOPK
    return pl.pallas_call(
        _route_kernel,
        grid=(n_tok // tt,),
        in_specs=[
            pl.BlockSpec((tt, d), lambda i: (i, 0)),
            pl.BlockSpec(wk.shape, lambda i: (0, 0)),
        ],
        out_specs=[pl.BlockSpec((tt, hk), lambda i: (i, 0)),
                   pl.BlockSpec((tt, hk), lambda i: (i, 0))],
        out_shape=[jax.ShapeDtypeStruct((n_tok, hk), jnp.int32),
                   jax.ShapeDtypeStruct((n_tok, hk), F32)],
        scratch_shapes=[pltpu.VMEM((hk, tt), jnp.int32), pltpu.VMEM((hk, tt), F32),
                        pltpu.VMEM((2 * PEER_HEADS, N_KEYS, tt), F32), pltpu.VMEM((2, 4 * TOPK, tt), F32)],
        compiler_params=pltpu.CompilerParams(
            dimension_semantics=("parallel",), vmem_limit_bytes=VMEM_LIMIT),
        name="peer_route",
    )(xt, wk)


TOK_GROUP = 128


def _pack_kernel(w_ref, o_ref):
    n = w_ref.shape[0]
    for s in range(ROW_WORDS):
        lo = w_ref[:, 2 * s * LANES:(2 * s + 1) * LANES].astype(BF16).astype(F32)
        hi = w_ref[:, (2 * s + 1) * LANES:(2 * s + 2) * LANES].astype(BF16).astype(F32)
        word = pltpu.bitcast(hi, jnp.uint32) | (pltpu.bitcast(lo, jnp.uint32) >> 16)
        o_ref[pl.ds(s, n, stride=ROW_WORDS), :] = word


def _pack_rows(w, rows):
    n, d = w.shape
    assert d == 2 * ROW_WORDS * LANES and n % rows == 0
    return pl.pallas_call(
        _pack_kernel,
        grid=(n // rows,),
        in_specs=[pl.BlockSpec((rows, d), lambda i: (i, 0))],
        out_specs=pl.BlockSpec((rows * ROW_WORDS, LANES), lambda i: (i, 0)),
        out_shape=jax.ShapeDtypeStruct((n * ROW_WORDS, LANES), jnp.uint32),
        compiler_params=pltpu.CompilerParams(
            dimension_semantics=("parallel",), vmem_limit_bytes=VMEM_LIMIT),
        name="pack_rows",
    )(w)


def _gather_rows(idx_ref, t, tab_ref, hk):
    words = [tab_ref[pl.ds(pl.multiple_of(idx_ref[t, k], ROW_WORDS), ROW_WORDS), :] for k in range(hk)]
    return pltpu.bitcast(jnp.concatenate(words, axis=0), BF16)


def _diag_mask(hk):
    r = lax.broadcasted_iota(jnp.int32, (SUBLANES, hk * SUBLANES), 0)
    c = lax.broadcasted_iota(jnp.int32, (SUBLANES, hk * SUBLANES), 1)
    return (c & (SUBLANES - 1)) == r


def _idx_copy(idx_hbm, row0, buf, sem):
    return pltpu.make_async_copy(idx_hbm.at[pl.ds(row0, TOK_GROUP)], buf, sem)


def _for_each_group(idx_hbm, tb, idx_a, idx_b, sems, process):
    step = pl.program_id(0)
    base = step * tb
    last = idx_hbm.shape[0] - TOK_GROUP

    @pl.when(step == 0)
    def _():
        _idx_copy(idx_hbm, 0, idx_a, sems.at[0]).start()

    def pair(i, carry):
        t0 = pl.multiple_of(i * (2 * TOK_GROUP), 2 * TOK_GROUP)
        _idx_copy(idx_hbm, base + t0 + TOK_GROUP, idx_b, sems.at[1]).start()
        _idx_copy(idx_hbm, 0, idx_a, sems.at[0]).wait()
        process(idx_a, t0, 0)
        _idx_copy(idx_hbm, jnp.minimum(base + t0 + 2 * TOK_GROUP, last), idx_a, sems.at[0]).start()
        _idx_copy(idx_hbm, 0, idx_b, sems.at[1]).wait()
        process(idx_b, t0 + TOK_GROUP, 1)
        return carry

    lax.fori_loop(0, tb // (2 * TOK_GROUP), pair, 0)

    @pl.when(step == pl.num_programs(0) - 1)
    def _():
        _idx_copy(idx_hbm, 0, idx_a, sems.at[0]).wait()


def _idx_scratch(hk):
    return [pltpu.SMEM((TOK_GROUP, hk), jnp.int32), pltpu.SMEM((TOK_GROUP, hk), jnp.int32),
            pltpu.SemaphoreType.DMA((2,))]


def _down_kernel(idx_hbm, x_ref, gate_ref, fold_ref, tab_ref, c_ref, idx_a, idx_b, sems, xt_ref,
                 part_a, part_b):
    hk = gate_ref.shape[1]
    tb = x_ref.shape[0]
    mask = _diag_mask(hk)
    parts = (part_a, part_b)

    def finish(part_ref, t0):
        z = _dot(part_ref[...].astype(BF16), fold_ref[...])
        act = jnp.concatenate(
            [jnp.sum(z[j * SUBLANES:(j + 1) * SUBLANES], axis=0, keepdims=True)
             for j in range(TOK_GROUP)], axis=0)
        c_ref[pl.ds(t0, TOK_GROUP), :] = gate_ref[pl.ds(t0, TOK_GROUP), :] * _gelu(act)

    def process(idx_ref, t0, slot):
        finish(parts[1 - slot], pl.multiple_of(jnp.maximum(t0 - TOK_GROUP, 0), TOK_GROUP))
        xs = x_ref[pl.ds(t0, TOK_GROUP), :]
        for r in range(SUBLANES):
            xt_ref[pl.ds(r, TOK_GROUP, stride=SUBLANES), :] = xs[:, r * LANES:(r + 1) * LANES]
        for j in range(TOK_GROUP):
            rows = _gather_rows(idx_ref, j, tab_ref, hk)
            xj = xt_ref[j * SUBLANES:(j + 1) * SUBLANES, :].astype(BF16)
            s = _dot_nt(xj, rows)
            parts[slot][j * SUBLANES:(j + 1) * SUBLANES, :] = jnp.where(mask, s, 0.0)

    part_b[...] = jnp.zeros(part_b.shape, F32)
    _for_each_group(idx_hbm, tb, idx_a, idx_b, sems, process)
    finish(part_b, tb - TOK_GROUP)


def _peer_down(idx, xt, gate, tab, tb):
    n_tok, hk = idx.shape
    assert tb % (2 * TOK_GROUP) == 0 and xt.shape[1] == SUBLANES * LANES
    fold = (jnp.arange(hk * SUBLANES)[:, None] // SUBLANES == jnp.arange(hk)[None, :]).astype(BF16)
    return pl.pallas_call(
        _down_kernel,
        grid=(n_tok // tb,),
        in_specs=[
            pl.BlockSpec(memory_space=pl.ANY),
            pl.BlockSpec((tb, SUBLANES * LANES), lambda i: (i, 0)),
            pl.BlockSpec((tb, hk), lambda i: (i, 0)),
            pl.BlockSpec(fold.shape, lambda i: (0, 0)),
            pl.BlockSpec(memory_space=pltpu.VMEM),
        ],
        out_specs=pl.BlockSpec((tb, hk), lambda i: (i, 0)),
        out_shape=jax.ShapeDtypeStruct((n_tok, hk), F32),
        scratch_shapes=_idx_scratch(hk) + [pltpu.VMEM((TOK_GROUP * SUBLANES, LANES), F32)]
        + [pltpu.VMEM((TOK_GROUP * SUBLANES, hk * SUBLANES), F32) for _ in range(2)],
        compiler_params=pltpu.CompilerParams(
            dimension_semantics=("arbitrary",), vmem_limit_bytes=VMEM_LIMIT),
        name="peer_down",
    )(idx, xt, gate, fold, tab)


def _up_kernel(idx_hbm, c_ref, x_ref, spread_ref, tab_ref, g_ref, b_ref, o_ref, idx_a, idx_b, sems,
               ht_a, ht_b, *, alpha):
    hk = c_ref.shape[1]
    tb = c_ref.shape[0]
    mask = _diag_mask(hk)
    hts = (ht_a, ht_b)

    def finish(ht_ref, t0):
        h = jnp.concatenate([ht_ref[pl.ds(r, TOK_GROUP, stride=SUBLANES), :] for r in range(SUBLANES)],
                            axis=-1)
        o_ref[pl.ds(t0, TOK_GROUP), :] = _ln(alpha * x_ref[pl.ds(t0, TOK_GROUP), :] + h,
                                              g_ref[...], b_ref[...])

    def process(idx_ref, t0, slot):
        finish(hts[1 - slot], pl.multiple_of(jnp.maximum(t0 - TOK_GROUP, 0), TOK_GROUP))
        cexp = _dot(c_ref[pl.ds(t0, TOK_GROUP), :].astype(BF16), spread_ref[...])
        for j in range(TOK_GROUP):
            rows = _gather_rows(idx_ref, j, tab_ref, hk)
            cj = jnp.where(mask, jnp.broadcast_to(cexp[j:j + 1, :], mask.shape), 0.0)
            hts[slot][j * SUBLANES:(j + 1) * SUBLANES, :] = _dot(cj.astype(BF16), rows)

    ht_b[...] = jnp.zeros(ht_b.shape, F32)
    _for_each_group(idx_hbm, tb, idx_a, idx_b, sems, process)
    finish(ht_b, tb - TOK_GROUP)


def _peer_up(idx, c, xt, tab, ln_g, ln_b, alpha, tb):
    n_tok, hk = idx.shape
    d = xt.shape[1]
    assert tb % (2 * TOK_GROUP) == 0 and d == SUBLANES * LANES
    spread = (jnp.arange(hk)[:, None] == jnp.arange(hk * SUBLANES)[None, :] // SUBLANES).astype(BF16)
    return pl.pallas_call(
        functools.partial(_up_kernel, alpha=alpha),
        grid=(n_tok // tb,),
        in_specs=[
            pl.BlockSpec(memory_space=pl.ANY),
            pl.BlockSpec((tb, hk), lambda i: (i, 0)),
            pl.BlockSpec((tb, d), lambda i: (i, 0)),
            pl.BlockSpec(spread.shape, lambda i: (0, 0)),
            pl.BlockSpec(memory_space=pltpu.VMEM),
            pl.BlockSpec((1, d), lambda i: (0, 0)),
            pl.BlockSpec((1, d), lambda i: (0, 0)),
        ],
        out_specs=pl.BlockSpec((tb, d), lambda i: (i, 0)),
        out_shape=jax.ShapeDtypeStruct((n_tok, d), F32),
        scratch_shapes=_idx_scratch(hk)
        + [pltpu.VMEM((TOK_GROUP * SUBLANES, LANES), F32) for _ in range(2)],
        compiler_params=pltpu.CompilerParams(
            dimension_semantics=("arbitrary",), vmem_limit_bytes=VMEM_LIMIT),
        name="peer_up",
    )(idx, c, xt, spread, tab, ln_g.reshape(1, -1), ln_b.reshape(1, -1))


def _tile(n, want):
    return want if n % want == 0 else n


def kernel(x, mem, w_in, b_in, sgu_g, sgu_b, w_spatial, b_spatial, conv_w, w_mix_out, ln1_g, ln1_b,
           w_xq, w_xkv, w_xo, ln2_g, ln2_b, w_pq, sub_keys, w_down, w_up, ln3_g, ln3_b):
    depth = w_in.shape[0]
    bsz, seq, d = x.shape
    assert d == SUBLANES * LANES and seq % CHUNK == 0
    alpha = np.float32((2 * depth) ** 0.25)
    n_tok = bsz * seq
    ts = _tile(seq, 512)
    for l in range(depth):
        qk, vo = _fold_mem(mem, w_xkv[l], w_xq[l], w_xo[l])
        x = _mixer(x, w_in[l], b_in[l], sgu_g[l], sgu_b[l], w_spatial[l], b_spatial[l], conv_w[l],
                   w_mix_out[l], ln1_g[l], ln1_b[l], qk, vo, ln2_g[l], ln2_b[l], alpha, ts)
        xt = x.reshape(n_tok, d)
        idx, gate = _peer_route(xt, w_pq[l], sub_keys[l], _tile(n_tok, 1024))
        tb = _tile(n_tok, 1024)
        pack = _tile(w_down.shape[1], 512)
        c = _peer_down(idx, xt, gate, _pack_rows(w_down[l], pack), tb)
        x = _peer_up(idx, c, xt, _pack_rows(w_up[l], pack), ln3_g[l], ln3_b[l], alpha, tb)
        x = x.reshape(bsz, seq, d)
    return x
```
